```python
import math
import jax, jax.numpy as jnp
from jax import lax
import numpy as np

D_MODEL = 1024
BATCH = 4
SEQ = 8192
DEPTH = 2

D_FF = 2816
ALPHA = (2 * DEPTH) ** 0.25
BETA = (8 * DEPTH) ** -0.25
LN_EPS = 1e-5
NEG = -1e30

A_HEADS = 8
A_DH = 64
A_WIDTH = A_HEADS * A_DH
MOBA_BLOCK = 256
MOBA_TOPK = 3
MOBA_QCHUNK = 64
ROPE_THETA = 10000.0

B_GROUPS = 8
B_DG = 64
B_WIDTH = B_GROUPS * B_DG
SGU_CHUNK = 128
AB_IN = 3 * A_WIDTH + 2 * B_WIDTH
AB_MIX = A_WIDTH + B_WIDTH

C_HEADS = 4
C_DQK = 128
C_DV = 256
C_QK_WIDTH = C_HEADS * C_DQK
C_WIDTH = C_HEADS * C_DV
C_IN = 2 * C_QK_WIDTH + 2 * C_WIDTH + 2 * C_HEADS
MLSTM_CHUNK = 64
CONV_W = 4

kernel_name = "hybrid_moba_sgu_mlstm_macaron_deepnorm"


def layer_norm(x, g, b):
    xf = x.astype(jnp.float32)
    mu = jnp.mean(xf, axis=-1, keepdims=True)
    var = jnp.mean(jnp.square(xf - mu), axis=-1, keepdims=True)
    y = (xf - mu) * lax.rsqrt(var + LN_EPS) * g.astype(jnp.float32) + b.astype(jnp.float32)
    return y.astype(x.dtype)


def swiglu(x, w_gu, w_down):
    g, u = jnp.split(x @ w_gu, 2, axis=-1)
    return (jax.nn.silu(g) * u) @ w_down


def rotary(x, pos):
    half = x.shape[-1] // 2
    inv = ROPE_THETA ** (-jnp.arange(half, dtype=jnp.float32) / half)
    ang = pos.astype(jnp.float32)[:, None] * inv[None, :]
    cos, sin = jnp.cos(ang), jnp.sin(ang)
    xf = x.astype(jnp.float32)
    x1, x2 = xf[..., :half], xf[..., half:]
    return jnp.concatenate([x1 * cos - x2 * sin, x2 * cos + x1 * sin], axis=-1).astype(x.dtype)


def moba_attention(q, k, v):
    Bn, H, S, dh = q.shape
    nb = -(-S // MOBA_BLOCK)
    pad = nb * MOBA_BLOCK - S
    kb = jnp.pad(k, ((0, 0), (0, 0), (0, pad), (0, 0))).reshape(Bn, H, nb, MOBA_BLOCK, dh)
    vb = jnp.pad(v, ((0, 0), (0, 0), (0, pad), (0, 0))).reshape(Bn, H, nb, MOBA_BLOCK, dh)
    k_mean = jnp.mean(kb.astype(jnp.float32), axis=3).astype(q.dtype)
    pos = jnp.arange(S)
    q_blk = pos // MOBA_BLOCK
    gate = jnp.einsum('bhsd,bhnd->bhsn', q, k_mean).astype(jnp.float32)
    past = jnp.arange(nb)[None, :] < q_blk[:, None]
    gate = jnp.where(past, gate, NEG)
    k_sel = min(MOBA_TOPK, nb)
    _, g_idx = lax.top_k(gate, k_sel)
    g_valid = g_idx < q_blk[None, None, :, None]
    scale = dh ** -0.5
    bi = jnp.arange(Bn)[:, None, None, None]
    hi = jnp.arange(H)[None, :, None, None]

    def step(c):
        start = c * MOBA_QCHUNK
        qc = lax.dynamic_slice_in_dim(q, start, MOBA_QCHUNK, axis=2)
        idx = lax.dynamic_slice_in_dim(g_idx, start, MOBA_QCHUNK, axis=2)
        val = lax.dynamic_slice_in_dim(g_valid, start, MOBA_QCHUNK, axis=2)
        blk = start // MOBA_BLOCK
        k_own = lax.dynamic_index_in_dim(kb, blk, axis=2, keepdims=False)
        v_own = lax.dynamic_index_in_dim(vb, blk, axis=2, keepdims=False)
        k_g = kb[bi, hi, idx]
        v_g = vb[bi, hi, idx]
        q_pos = start + jnp.arange(MOBA_QCHUNK)
        k_pos = blk * MOBA_BLOCK + jnp.arange(MOBA_BLOCK)
        s_own = jnp.einsum('bhqd,bhkd->bhqk', qc, k_own).astype(jnp.float32) * scale
        s_own = jnp.where(k_pos[None, :] <= q_pos[:, None], s_own, NEG)
        s_g = jnp.einsum('bhqd,bhqnkd->bhqnk', qc, k_g).astype(jnp.float32) * scale
        s_g = jnp.where(val[..., None], s_g, NEG)
        s = jnp.concatenate([s_own, s_g.reshape(Bn, H, MOBA_QCHUNK, k_sel * MOBA_BLOCK)], axis=-1)
        p = jax.nn.softmax(s, axis=-1).astype(v.dtype)
        p_own = p[..., :MOBA_BLOCK]
        p_g = p[..., MOBA_BLOCK:].reshape(Bn, H, MOBA_QCHUNK, k_sel, MOBA_BLOCK)
        return (jnp.einsum('bhqk,bhkd->bhqd', p_own, v_own)
                + jnp.einsum('bhqnk,bhqnkd->bhqd', p_g, v_g))

    outs = lax.map(step, jnp.arange(S // MOBA_QCHUNK))
    return jnp.moveaxis(outs, 0, 2).reshape(Bn, H, S, dh)


def spatial_gating(u, vg, ln_g, ln_b, w_s, b_s):
    Bn, S, _ = u.shape
    vg = layer_norm(vg, ln_g, ln_b)
    nc = S // SGU_CHUNK
    vr = vg.reshape(Bn, nc, SGU_CHUNK, B_GROUPS, B_DG)
    causal = jnp.tril(jnp.ones((SGU_CHUNK, SGU_CHUNK), dtype=bool))
    w = jnp.where(causal, w_s, jnp.zeros_like(w_s))
    mixed = jnp.einsum('gts,bcsgd->bctgd', w, vr) + b_s.T[None, None, :, :, None]
    return u * mixed.reshape(Bn, S, B_WIDTH)


def mixer_ab(x, w_in, sgu_ln_g, sgu_ln_b, sgu_w, sgu_b, w_out):
    Bn, S, _ = x.shape
    proj = x @ w_in
    qa, ka, va, ub, vb = jnp.split(
        proj, [A_WIDTH, 2 * A_WIDTH, 3 * A_WIDTH, 3 * A_WIDTH + B_WIDTH], axis=-1)
    heads = lambda t: t.reshape(Bn, S, A_HEADS, A_DH).transpose(0, 2, 1, 3)
    pos = jnp.arange(S)
    a = moba_attention(rotary(heads(qa), pos), rotary(heads(ka), pos), heads(va))
    a = a.transpose(0, 2, 1, 3).reshape(Bn, S, A_WIDTH)
    b = spatial_gating(jax.nn.gelu(ub), jax.nn.gelu(vb), sgu_ln_g, sgu_ln_b, sgu_w, sgu_b)
    return jnp.concatenate([a, b], axis=-1) @ w_out


def causal_depthwise_conv(x, w, b):
    K, C = w.shape
    y = lax.conv_general_dilated(x, w[:, None, :], window_strides=(1,), padding=[(K - 1, 0)],
                                 dimension_numbers=('NWC', 'WIO', 'NWC'), feature_group_count=C)
    return y + b


def mlstm_chunkwise(q, k, v, i_pre, f_pre):
    Bn, H, S, dk = q.shape
    dv = v.shape[-1]
    L = MLSTM_CHUNK
    nc = S // L
    f32 = jnp.float32
    qc = q.astype(f32).reshape(Bn, H, nc, L, dk)
    kc = (k.astype(f32) * dk ** -0.5).reshape(Bn, H, nc, L, dk)
    vc = v.astype(f32).reshape(Bn, H, nc, L, dv)
    ic = i_pre.reshape(Bn, H, nc, L)
    b = jnp.cumsum(jax.nn.log_sigmoid(f_pre).reshape(Bn, H, nc, L), axis=-1)
    b_end = b[..., -1]
    a = b_end[..., None] - b + ic
    a_max = jnp.max(a, axis=-1)
    w_end = jnp.exp(a - a_max[..., None])
    kv = jnp.einsum('bhcsk,bhcsv,bhcs->bhckv', kc, vc, w_end)
    ks = jnp.einsum('bhcsk,bhcs->bhck', kc, w_end)

    def scan_fn(carry, inp):
        C, n, m = carry
        kv_c, ks_c, be_c, am_c = inp
        m_new = jnp.maximum(be_c + m, am_c)
        decay = jnp.exp(be_c + m - m_new)
        inject = jnp.exp(am_c - m_new)
        C_new = decay[..., None, None] * C + inject[..., None, None] * kv_c
        n_new = decay[..., None] * n + inject[..., None] * ks_c
        return (C_new, n_new, m_new), (C, n, m)

    init = (jnp.zeros((Bn, H, dk, dv), f32), jnp.zeros((Bn, H, dk), f32), jnp.zeros((Bn, H), f32))
    xs = (jnp.moveaxis(kv, 2, 0), jnp.moveaxis(ks, 2, 0), jnp.moveaxis(b_end, 2, 0), jnp.moveaxis(a_max, 2, 0))
    _, (C_in, n_in, m_in) = lax.scan(scan_fn, init, xs)
    C_in = jnp.moveaxis(C_in, 0, 2)
    n_in = jnp.moveaxis(n_in, 0, 2)
    m_in = jnp.moveaxis(m_in, 0, 2)

    causal = jnp.tril(jnp.ones((L, L), dtype=bool))
    D = b[..., :, None] - b[..., None, :] + ic[..., None, :]
    D = jnp.where(causal, D, -jnp.inf)
    g = b + m_in[..., None]
    m_t = jnp.maximum(g, jnp.max(D, axis=-1))
    P = jnp.exp(D - m_t[..., None])
    sqk = jnp.einsum('bhctk,bhcsk->bhcts', qc, kc) * P
    inter = jnp.exp(g - m_t)
    num = (jnp.einsum('bhcts,bhcsv->bhctv', sqk, vc)
           + inter[..., None] * jnp.einsum('bhctk,bhckv->bhctv', qc, C_in))
    den = jnp.sum(sqk, axis=-1) + inter * jnp.einsum('bhctk,bhck->bhct', qc, n_in)
    h = num / jnp.maximum(jnp.abs(den), jnp.exp(-m_t))[..., None]
    return h.reshape(Bn, H, S, dv)


def mixer_c(x, w_in, conv_w, conv_b, b_i, b_f, head_g, w_out):
    Bn, S, _ = x.shape
    proj = x @ w_in
    qk, v, o_pre, if_pre = jnp.split(
        proj, [2 * C_QK_WIDTH, 2 * C_QK_WIDTH + C_WIDTH, 2 * C_QK_WIDTH + 2 * C_WIDTH], axis=-1)
    qk = jax.nn.silu(causal_depthwise_conv(qk, conv_w, conv_b))
    q, k = jnp.split(qk, 2, axis=-1)
    q = q.reshape(Bn, S, C_HEADS, C_DQK).transpose(0, 2, 1, 3)
    k = k.reshape(Bn, S, C_HEADS, C_DQK).transpose(0, 2, 1, 3)
    v = v.reshape(Bn, S, C_HEADS, C_DV).transpose(0, 2, 1, 3)
    if_f = if_pre.astype(jnp.float32)
    i_pre = (if_f[..., :C_HEADS] + b_i.astype(jnp.float32)).transpose(0, 2, 1)
    f_pre = (if_f[..., C_HEADS:] + b_f.astype(jnp.float32)).transpose(0, 2, 1)
    h = mlstm_chunkwise(q, k, v, i_pre, f_pre)
    mu = jnp.mean(h, axis=-1, keepdims=True)
    var = jnp.mean(jnp.square(h - mu), axis=-1, keepdims=True)
    h = (h - mu) * lax.rsqrt(var + LN_EPS) * head_g.astype(jnp.float32).reshape(C_HEADS, 1, C_DV)
    h = h.transpose(0, 2, 1, 3).reshape(Bn, S, C_WIDTH).astype(x.dtype)
    return (jax.nn.sigmoid(o_pre) * h) @ w_out


def setup_inputs(seed: int = 0) -> dict:
    key = jax.random.key(seed)
    ks = jax.random.split(key, 24)
    n_even = (DEPTH + 1) // 2
    n_odd = DEPTH // 2
    nrm = lambda kk, shape, s: s * jax.random.normal(kk, shape, jnp.float32)
    x = nrm(ks[0], (BATCH, SEQ, D_MODEL), 1.0)
    ln_g = 1.0 + nrm(ks[1], (DEPTH, 3, D_MODEL), 0.02)
    ln_b = nrm(ks[2], (DEPTH, 3, D_MODEL), 0.02)
    ffn_w_gu = nrm(ks[3], (DEPTH, 2, D_MODEL, 2 * D_FF), D_MODEL ** -0.5)
    ffn_w_down = nrm(ks[4], (DEPTH, 2, D_FF, D_MODEL), BETA * D_FF ** -0.5)
    ab_cols = jnp.concatenate([jnp.ones((2 * A_WIDTH,), jnp.float32),
                               jnp.full((A_WIDTH,), BETA, jnp.float32),
                               jnp.ones((2 * B_WIDTH,), jnp.float32)])
    ab_w_in = nrm(ks[5], (n_even, D_MODEL, AB_IN), D_MODEL ** -0.5) * ab_cols
    sgu_ln_g = 1.0 + nrm(ks[6], (n_even, B_WIDTH), 0.02)
    sgu_ln_b = nrm(ks[7], (n_even, B_WIDTH), 0.02)
    sgu_w = nrm(ks[8], (n_even, B_GROUPS, SGU_CHUNK, SGU_CHUNK), SGU_CHUNK ** -0.5)
    sgu_b = 1.0 + nrm(ks[9], (n_even, B_GROUPS, SGU_CHUNK), 0.02)
    ab_w_out = nrm(ks[10], (n_even, AB_MIX, D_MODEL), BETA * AB_MIX ** -0.5)
    c_cols = jnp.concatenate([jnp.ones((2 * C_QK_WIDTH,), jnp.float32),
                              jnp.full((C_WIDTH,), BETA, jnp.float32),
                              jnp.ones((C_WIDTH + 2 * C_HEADS,), jnp.float32)])
    c_w_in = nrm(ks[11], (n_odd, D_MODEL, C_IN), D_MODEL ** -0.5) * c_cols
    c_conv_w = nrm(ks[12], (n_odd, CONV_W, 2 * C_QK_WIDTH), CONV_W ** -0.5)
    c_conv_b = nrm(ks[13], (n_odd, 2 * C_QK_WIDTH), 0.02)
    c_b_i = nrm(ks[14], (n_odd, C_HEADS), 0.1)
    c_b_f = jnp.linspace(3.0, 6.0, C_HEADS, dtype=jnp.float32)[None, :] + nrm(ks[15], (n_odd, C_HEADS), 0.1)
    c_head_g = 1.0 + nrm(ks[16], (n_odd, C_WIDTH), 0.02)
    c_w_out = nrm(ks[17], (n_odd, C_WIDTH, D_MODEL), BETA * C_WIDTH ** -0.5)
    return {"x": x, "ln_g": ln_g, "ln_b": ln_b, "ffn_w_gu": ffn_w_gu, "ffn_w_down": ffn_w_down,
            "ab_w_in": ab_w_in, "sgu_ln_g": sgu_ln_g, "sgu_ln_b": sgu_ln_b, "sgu_w": sgu_w,
            "sgu_b": sgu_b, "ab_w_out": ab_w_out, "c_w_in": c_w_in, "c_conv_w": c_conv_w,
            "c_conv_b": c_conv_b, "c_b_i": c_b_i, "c_b_f": c_b_f, "c_head_g": c_head_g,
            "c_w_out": c_w_out}


def reference(x, ln_g, ln_b, ffn_w_gu, ffn_w_down, ab_w_in, sgu_ln_g, sgu_ln_b, sgu_w, sgu_b,
              ab_w_out, c_w_in, c_conv_w, c_conv_b, c_b_i, c_b_f, c_head_g, c_w_out):
    for l in range(DEPTH):
        x = layer_norm(ALPHA * x + 0.5 * swiglu(x, ffn_w_gu[l, 0], ffn_w_down[l, 0]), ln_g[l, 0], ln_b[l, 0])
        j = l // 2
        if l % 2 == 0:
            y = mixer_ab(x, ab_w_in[j], sgu_ln_g[j], sgu_ln_b[j], sgu_w[j], sgu_b[j], ab_w_out[j])
        else:
            y = mixer_c(x, c_w_in[j], c_conv_w[j], c_conv_b[j], c_b_i[j], c_b_f[j], c_head_g[j], c_w_out[j])
        x = layer_norm(ALPHA * x + y, ln_g[l, 1], ln_b[l, 1])
        x = layer_norm(ALPHA * x + 0.5 * swiglu(x, ffn_w_gu[l, 1], ffn_w_down[l, 1]), ln_g[l, 2], ln_b[l, 2])
    return x
```

```python
import functools

import jax
import jax.numpy as jnp
from jax import lax
from jax.experimental import pallas as pl
from jax.experimental.pallas import tpu as pltpu

D_MODEL = 1024
DEPTH = 2
D_FF = 2816
ALPHA = (2 * DEPTH) ** 0.25
LN_EPS = 1e-5
NEG = -1e30

A_HEADS = 8
A_DH = 64
A_WIDTH = A_HEADS * A_DH
MOBA_BLOCK = 256
MOBA_TOPK = 3
ROPE_THETA = 10000.0

B_GROUPS = 8
B_DG = 64
B_WIDTH = B_GROUPS * B_DG
SGU_CHUNK = 128

C_HEADS = 4
C_DQK = 128
C_DV = 256
C_QK_WIDTH = C_HEADS * C_DQK
C_WIDTH = C_HEADS * C_DV
CONV_W = 4

LANES = 128
SUBLANES = 8
VMEM_LIMIT = 56 * 1024 * 1024

TOKEN_TILE = 512
FF_CHUNK = 256
MLSTM_L = 256

BF16 = jnp.bfloat16
F32 = jnp.float32


def _params(sem):
    return pltpu.CompilerParams(dimension_semantics=sem, vmem_limit_bytes=VMEM_LIMIT)


def _dot(a, b):
    return jnp.dot(a, b, preferred_element_type=F32)


def _dot_nt(a, b):
    return lax.dot_general(a, b, (((1,), (1,)), ((), ())), preferred_element_type=F32)


def _dot_tn(a, b):
    return lax.dot_general(a, b, (((0,), (0,)), ((), ())), preferred_element_type=F32)


def _layer_norm(z, g, b):
    mu = jnp.mean(z, axis=-1, keepdims=True)
    zc = z - mu
    var = jnp.mean(zc * zc, axis=-1, keepdims=True)
    y = zc * lax.rsqrt(var + LN_EPS) * g
    return y if b is None else y + b


def _ffn_ln_kernel(x_ref, wgu_ref, wd_ref, g_ref, b_ref, o_ref, acc_ref):
    x = x_ref[...]
    xb = x.astype(BF16)
    for c in range(D_FF // FF_CHUNK):
        lo = c * FF_CHUNK
        gate = _dot(xb, wgu_ref[:, lo:lo + FF_CHUNK])
        up = _dot(xb, wgu_ref[:, D_FF + lo:D_FF + lo + FF_CHUNK])
        h = (jax.nn.silu(gate) * up).astype(BF16)
        part = _dot(h, wd_ref[lo:lo + FF_CHUNK, :])
        if c == 0:
            acc_ref[...] = part
        else:
            acc_ref[...] += part
    z = ALPHA * x + 0.5 * acc_ref[...]
    o_ref[...] = _layer_norm(z, g_ref[...], b_ref[...])


def _ffn_ln(x2, w_gu, w_down, g, b):
    n = x2.shape[0]
    tm = TOKEN_TILE
    const = lambda i: (0, 0)
    return pl.pallas_call(
        _ffn_ln_kernel,
        grid=(n // tm,),
        in_specs=[
            pl.BlockSpec((tm, D_MODEL), lambda i: (i, 0)),
            pl.BlockSpec((D_MODEL, 2 * D_FF), const),
            pl.BlockSpec((D_FF, D_MODEL), const),
            pl.BlockSpec((1, D_MODEL), const),
            pl.BlockSpec((1, D_MODEL), const),
        ],
        out_specs=pl.BlockSpec((tm, D_MODEL), lambda i: (i, 0)),
        out_shape=jax.ShapeDtypeStruct((n, D_MODEL), F32),
        scratch_shapes=[pltpu.VMEM((tm, D_MODEL), F32)],
        compiler_params=_params(("arbitrary",)),
    )(x2, w_gu, w_down, g, b)


def _rotate_half_pairs(t, first_half):
    fwd = pltpu.roll(t, A_DH // 2, axis=1)
    bwd = pltpu.roll(t, LANES - A_DH // 2, axis=1)
    return jnp.where(first_half, bwd, fwd)


def _proj_ab_kernel(x_ref, w_ref, cos_ref, sin_ref, lng_ref, lnb_ref,
                    q_ref, k_ref, v_ref, u_ref, vn_ref):
    xb = x_ref[0].astype(BF16)
    cos = cos_ref[...]
    sin = sin_ref[...]
    lane = lax.broadcasted_iota(jnp.int32, cos.shape, 1)
    first_half = (lane % A_DH) < (A_DH // 2)
    sin_signed = jnp.where(first_half, -sin, sin)
    for grp in range(A_WIDTH // LANES):
        lo = grp * LANES
        q = _dot(xb, w_ref[:, lo:lo + LANES])
        k = _dot(xb, w_ref[:, A_WIDTH + lo:A_WIDTH + lo + LANES])
        q = q * cos + _rotate_half_pairs(q, first_half) * sin_signed
        k = k * cos + _rotate_half_pairs(k, first_half) * sin_signed
        q_ref[0, :, lo:lo + LANES] = (q * (A_DH ** -0.5)).astype(BF16)
        k_ref[0, :, lo:lo + LANES] = k.astype(BF16)
    v_ref[0] = _dot(xb, w_ref[:, 2 * A_WIDTH:3 * A_WIDTH]).astype(BF16)
    ub = _dot(xb, w_ref[:, 3 * A_WIDTH:3 * A_WIDTH + B_WIDTH])
    u_ref[0] = jax.nn.gelu(ub).astype(BF16)
    vb = _dot(xb, w_ref[:, 3 * A_WIDTH + B_WIDTH:])
    vn_ref[0] = _layer_norm(jax.nn.gelu(vb), lng_ref[...], lnb_ref[...]).astype(BF16)


def _proj_ab(x, w_in, cos, sin, lng, lnb):
    bn, s, _ = x.shape
    tm = TOKEN_TILE
    const = lambda b, t: (0, 0)
    tok = lambda b, t: (b, t, 0)
    out = jax.ShapeDtypeStruct((bn, s, A_WIDTH), BF16)
    return pl.pallas_call(
        _proj_ab_kernel,
        grid=(bn, s // tm),
        in_specs=[
            pl.BlockSpec((1, tm, D_MODEL), tok),
            pl.BlockSpec(w_in.shape, const),
            pl.BlockSpec((tm, LANES), lambda b, t: (t, 0)),
            pl.BlockSpec((tm, LANES), lambda b, t: (t, 0)),
            pl.BlockSpec((1, B_WIDTH), const),
            pl.BlockSpec((1, B_WIDTH), const),
        ],
        out_specs=[pl.BlockSpec((1, tm, A_WIDTH), tok)] * 5,
        out_shape=[out] * 5,
        compiler_params=_params(("arbitrary", "arbitrary")),
    )(x, w_in, cos, sin, lng, lnb)


def _moba_kernel(q_ref, k_ref, v_ref, o_ref, kmean_ref):
    i = pl.program_id(2)
    blk = MOBA_BLOCK
    nb = k_ref.shape[1] // blk
    lane = lax.broadcasted_iota(jnp.int32, (blk, LANES), 1)
    head_masks = (lane < A_DH, lane >= A_DH)
    lane_sq = lax.broadcasted_iota(jnp.int32, (LANES, LANES), 1)
    kmean_masks = (lane_sq < A_DH, lane_sq >= A_DH)

    @pl.when(i == 0)
    def _():
        kmean_ref[...] = jnp.zeros_like(kmean_ref)
        for j in range(nb):
            kj = k_ref[0, j * blk:(j + 1) * blk, :].astype(F32)
            kmean_ref[j:j + 1, :] = jnp.mean(kj, axis=0, keepdims=True)

    q_pair = q_ref[0]
    k_own = k_ref[0, pl.ds(pl.multiple_of(i * blk, blk), blk), :]
    v_own = v_ref[0, pl.ds(pl.multiple_of(i * blk, blk), blk), :]
    row = lax.broadcasted_iota(jnp.int32, (blk, blk), 0)
    col = lax.broadcasted_iota(jnp.int32, (blk, blk), 1)
    causal = col <= row
    kmean = kmean_ref[...]

    q_aug, m_run, l_run, acc = [], [], [], []
    for h in range(2):
        qh = jnp.where(head_masks[h], q_pair, jnp.zeros_like(q_pair))
        gate = _dot_nt(qh, jnp.where(kmean_masks[h], kmean, 0.0).astype(BF16))
        gate = jnp.where(lane < i, gate, NEG)
        chosen = jnp.zeros((blk, LANES), jnp.bool_)
        for _ in range(MOBA_TOPK):
            best = jnp.max(gate, axis=-1, keepdims=True)
            idx = jnp.min(jnp.where(gate == best, lane, LANES), axis=-1, keepdims=True)
            pick = lane == idx
            chosen = chosen | (pick & (idx < i))
            gate = jnp.where(pick, -jnp.inf, gate)
        bias = jnp.where(chosen, 0.0, NEG).astype(BF16)
        q_aug.append(jnp.concatenate([qh, bias], axis=1))

        s = jnp.where(causal, _dot_nt(qh, k_own), NEG)
        m0 = jnp.max(s, axis=-1, keepdims=True)
        p = jnp.exp(s - m0)
        vh = jnp.where(head_masks[h], v_own, jnp.zeros_like(v_own))
        m_run.append(m0)
        l_run.append(jnp.sum(p, axis=-1, keepdims=True))
        acc.append(_dot(p.astype(BF16), vh))

    def body(j, carry):
        start = pl.multiple_of(j * blk, blk)
        kj = k_ref[0, pl.ds(start, blk), :]
        vj = v_ref[0, pl.ds(start, blk), :]
        k_aug = jnp.concatenate([kj, (lane == j).astype(BF16)], axis=1)
        out = []
        for h in range(2):
            m_old, l_old, a_old = carry[3 * h], carry[3 * h + 1], carry[3 * h + 2]
            s = _dot_nt(q_aug[h], k_aug)
            m_new = jnp.maximum(m_old, jnp.max(s, axis=-1, keepdims=True))
            alpha = jnp.exp(m_old - m_new)
            p = jnp.exp(s - m_new)
            vh = jnp.where(head_masks[h], vj, jnp.zeros_like(vj))
            out += [m_new,
                    alpha * l_old + jnp.sum(p, axis=-1, keepdims=True),
                    alpha * a_old + _dot(p.astype(BF16), vh)]
        return tuple(out)

    carry = lax.fori_loop(0, i, body, (m_run[0], l_run[0], acc[0], m_run[1], l_run[1], acc[1]))
    o_ref[0] = (carry[2] / carry[1] + carry[5] / carry[4]).astype(BF16)


def _moba(q, k, v):
    bn, s, _ = q.shape
    blk = MOBA_BLOCK
    return pl.pallas_call(
        _moba_kernel,
        grid=(bn, A_WIDTH // LANES, s // blk),
        in_specs=[
            pl.BlockSpec((1, blk, LANES), lambda b, p, i: (b, i, p)),
            pl.BlockSpec((1, s, LANES), lambda b, p, i: (b, 0, p)),
            pl.BlockSpec((1, s, LANES), lambda b, p, i: (b, 0, p)),
        ],
        out_specs=pl.BlockSpec((1, blk, LANES), lambda b, p, i: (b, i, p)),
        out_shape=jax.ShapeDtypeStruct((bn, s, A_WIDTH), BF16),
        scratch_shapes=[pltpu.VMEM((LANES, LANES), F32)],
        compiler_params=_params(("arbitrary", "arbitrary", "arbitrary")),
    )(q, k, v)


def _sgu_kernel(u_ref, vn_ref, w_ref, bs_ref, o_ref):
    tm = u_ref.shape[1]
    t_row = lax.broadcasted_iota(jnp.int32, (SGU_CHUNK, SGU_CHUNK), 0)
    t_col = lax.broadcasted_iota(jnp.int32, (SGU_CHUNK, SGU_CHUNK), 1)
    causal = t_col <= t_row
    lane = lax.broadcasted_iota(jnp.int32, (SGU_CHUNK, LANES), 1)
    lo_mask = lane < B_DG
    for pair in range(B_WIDTH // LANES):
        w_lo = jnp.where(causal, w_ref[2 * pair], 0.0).astype(BF16)
        w_hi = jnp.where(causal, w_ref[2 * pair + 1], 0.0).astype(BF16)
        cols = slice(pair * LANES, (pair + 1) * LANES)
        bias = bs_ref[:, cols]
        for c in range(tm // SGU_CHUNK):
            rows = slice(c * SGU_CHUNK, (c + 1) * SGU_CHUNK)
            vp = vn_ref[0, rows, cols]
            zero = jnp.zeros_like(vp)
            mixed = _dot(w_lo, jnp.where(lo_mask, vp, zero)) + _dot(w_hi, jnp.where(lo_mask, zero, vp))
            o_ref[0, rows, cols] = (u_ref[0, rows, cols].astype(F32) * (mixed + bias)).astype(BF16)


def _sgu(u, vn, w_s, bs_full):
    bn, s, _ = u.shape
    tm = TOKEN_TILE
    tok = lambda b, t: (b, t, 0)
    return pl.pallas_call(
        _sgu_kernel,
        grid=(bn, s // tm),
        in_specs=[
            pl.BlockSpec((1, tm, B_WIDTH), tok),
            pl.BlockSpec((1, tm, B_WIDTH), tok),
            pl.BlockSpec(w_s.shape, lambda b, t: (0, 0, 0)),
            pl.BlockSpec(bs_full.shape, lambda b, t: (0, 0)),
        ],
        out_specs=pl.BlockSpec((1, tm, B_WIDTH), tok),
        out_shape=jax.ShapeDtypeStruct((bn, s, B_WIDTH), BF16),
        compiler_params=_params(("arbitrary", "arbitrary")),
    )(u, vn, w_s, bs_full)


def _out_ln_kernel(n_parts, *refs):
    x_ref = refs[0]
    part_refs = refs[1:1 + n_parts]
    w_ref, g_ref, b_ref, o_ref = refs[1 + n_parts:]
    y = None
    lo = 0
    for p_ref in part_refs:
        width = p_ref.shape[-1]
        term = _dot(p_ref[...], w_ref[lo:lo + width, :])
        y = term if y is None else y + term
        lo += width
    o_ref[...] = _layer_norm(ALPHA * x_ref[...] + y, g_ref[...], b_ref[...])


def _out_ln(x2, parts, w_out, g, b):
    n = x2.shape[0]
    tm = TOKEN_TILE
    const = lambda i: (0, 0)
    tok = lambda i: (i, 0)
    return pl.pallas_call(
        functools.partial(_out_ln_kernel, len(parts)),
        grid=(n // tm,),
        in_specs=[pl.BlockSpec((tm, D_MODEL), tok)]
        + [pl.BlockSpec((tm, p.shape[-1]), tok) for p in parts]
        + [pl.BlockSpec(w_out.shape, const), pl.BlockSpec((1, D_MODEL), const),
           pl.BlockSpec((1, D_MODEL), const)],
        out_specs=pl.BlockSpec((tm, D_MODEL), tok),
        out_shape=jax.ShapeDtypeStruct((n, D_MODEL), F32),
        compiler_params=_params(("arbitrary",)),
    )(x2, *parts, w_out, g, b)


def _proj_c_kernel(x_ref, wqk_ref, wv_ref, wo_ref, wif_ref, wift_ref, cw_ref, cb_ref, gb_ref, gbt_ref,
                   q_ref, k_ref, v_ref, og_ref, gates_ref, gatest_ref, tail_ref):
    t = pl.program_id(1)
    tm = x_ref.shape[1]
    xb = x_ref[0].astype(BF16)

    @pl.when(t == 0)
    def _():
        tail_ref[...] = jnp.zeros_like(tail_ref)

    qk = _dot(xb, wqk_ref[...])
    ext = jnp.concatenate([tail_ref[...], qk], axis=0)
    tail_ref[...] = qk[tm - SUBLANES:, :]
    conv = cb_ref[...] + cw_ref[CONV_W - 1:CONV_W, :] * qk
    for d in range(1, CONV_W):
        conv = conv + cw_ref[CONV_W - 1 - d:CONV_W - d, :] * ext[SUBLANES - d:SUBLANES - d + tm, :]
    act = jax.nn.silu(conv)
    q_ref[0] = act[:, :C_QK_WIDTH].astype(BF16)
    k_ref[0] = (act[:, C_QK_WIDTH:] * (C_DQK ** -0.5)).astype(BF16)
    v_ref[0] = _dot(xb, wv_ref[...]).astype(BF16)
    og_ref[0] = jax.nn.sigmoid(_dot(xb, wo_ref[...])).astype(BF16)
    gates_ref[0] = _dot(xb, wif_ref[...]) + gb_ref[...]
    gatest_ref[0] = _dot_nt(wift_ref[...], xb) + gbt_ref[...]


def _proj_c(x, wqk, wv, wo, wif, wift, conv_w, conv_b, gb, gbt):
    bn, s, _ = x.shape
    tm = TOKEN_TILE
    tok = lambda b, t: (b, t, 0)
    const = lambda b, t: (0, 0)
    full = lambda a: pl.BlockSpec(a.shape, const)
    return pl.pallas_call(
        _proj_c_kernel,
        grid=(bn, s // tm),
        in_specs=[pl.BlockSpec((1, tm, D_MODEL), tok)]
        + [full(a) for a in (wqk, wv, wo, wif, wift, conv_w, conv_b, gb, gbt)],
        out_specs=[
            pl.BlockSpec((1, tm, C_QK_WIDTH), tok),
            pl.BlockSpec((1, tm, C_QK_WIDTH), tok),
            pl.BlockSpec((1, tm, C_WIDTH), tok),
            pl.BlockSpec((1, tm, C_WIDTH), tok),
            pl.BlockSpec((1, tm, LANES), tok),
            pl.BlockSpec((1, SUBLANES, tm), lambda b, t: (b, 0, t)),
        ],
        out_shape=[
            jax.ShapeDtypeStruct((bn, s, C_QK_WIDTH), BF16),
            jax.ShapeDtypeStruct((bn, s, C_QK_WIDTH), BF16),
            jax.ShapeDtypeStruct((bn, s, C_WIDTH), BF16),
            jax.ShapeDtypeStruct((bn, s, C_WIDTH), BF16),
            jax.ShapeDtypeStruct((bn, s, LANES), F32),
            jax.ShapeDtypeStruct((bn, SUBLANES, s), F32),
        ],
        scratch_shapes=[pltpu.VMEM((SUBLANES, 2 * C_QK_WIDTH), F32)],
        compiler_params=_params(("arbitrary", "arbitrary")),
    )(x, wqk, wv, wo, wif, wift, conv_w, conv_b, gb, gbt)


def _split3(a):
    hi = a.astype(BF16)
    r1 = a - hi.astype(F32)
    mid = r1.astype(BF16)
    lo = (r1 - mid.astype(F32)).astype(BF16)
    return hi, mid, lo


def _mlstm_kernel(q_ref, k_ref, v_ref, og_ref, gates_ref, gatest_ref, hg_ref, o_ref,
                  c_ref, n_ref, m_ref):
    c_idx = pl.program_id(1)
    L = q_ref.shape[1]

    @pl.when(c_idx == 0)
    def _():
        c_ref[...] = jnp.zeros_like(c_ref)
        n_ref[...] = jnp.zeros_like(n_ref)
        m_ref[...] = jnp.zeros_like(m_ref)

    t_row = lax.broadcasted_iota(jnp.int32, (L, L), 0)
    t_col = lax.broadcasted_iota(jnp.int32, (L, L), 1)
    causal = t_col <= t_row
    tri = causal.astype(BF16)
    tri_t = (t_row <= t_col).astype(BF16)

    gates = gates_ref[0]
    gates_t = gatest_ref[0]
    b_cols = sum(_dot(tri, part) for part in _split3(jax.nn.log_sigmoid(gates)))
    b_rows = sum(_dot(part, tri_t) for part in _split3(jax.nn.log_sigmoid(gates_t)))

    for h in range(C_HEADS):
        i_col = gates[:, h:h + 1]
        i_row = gates_t[h:h + 1, :]
        b_col = b_cols[:, C_HEADS + h:C_HEADS + h + 1]
        b_row = b_rows[C_HEADS + h:C_HEADS + h + 1, :]
        b_end = b_col[L - 1:L, :]
        m_in = m_ref[h][:, 0:1]

        a_col = b_end - b_col + i_col
        a_max = jnp.max(a_col, axis=0, keepdims=True)
        w_end = jnp.exp(a_col - a_max)

        dmat = jnp.where(causal, b_col - b_row + i_row, -jnp.inf)
        g_col = b_col + m_in
        m_t = jnp.maximum(g_col, jnp.max(dmat, axis=-1, keepdims=True))
        pmat = jnp.exp(dmat - m_t)
        inter = jnp.exp(g_col - m_t)

        qh = q_ref[0, :, h * C_DQK:(h + 1) * C_DQK]
        kh = k_ref[0, :, h * C_DQK:(h + 1) * C_DQK]
        vh = v_ref[0, :, h * C_DV:(h + 1) * C_DV]
        c_in = c_ref[h]
        n_in = n_ref[h]

        sqk = _dot_nt(qh, kh) * pmat
        num = _dot(sqk.astype(BF16), vh) + inter * _dot(qh, c_in.astype(BF16))
        den = (jnp.sum(sqk, axis=-1, keepdims=True)
               + inter * jnp.sum(qh.astype(F32) * n_in, axis=-1, keepdims=True))
        hid = num / jnp.maximum(jnp.abs(den), jnp.exp(-m_t))
        hid = _layer_norm(hid, hg_ref[:, h * C_DV:(h + 1) * C_DV], None)
        gate_o = og_ref[0, :, h * C_DV:(h + 1) * C_DV].astype(F32)
        o_ref[0, :, h * C_DV:(h + 1) * C_DV] = (gate_o * hid).astype(BF16)

        m_new = jnp.maximum(b_end + m_in, a_max)
        decay = jnp.exp(b_end + m_in - m_new)
        inject = jnp.exp(a_max - m_new)
        kw = kh.astype(F32) * w_end
        c_ref[h] = decay * c_in + inject * _dot_tn(kw.astype(BF16), vh)
        n_ref[h] = decay * n_in + inject * jnp.sum(kw, axis=0, keepdims=True)
        m_ref[h] = jnp.broadcast_to(m_new, (1, LANES))


def _mlstm(q, k, v, og, gates, gates_t, head_g):
    bn, s, _ = q.shape
    L = MLSTM_L
    tok = lambda b, c: (b, c, 0)
    return pl.pallas_call(
        _mlstm_kernel,
        grid=(bn, s // L),
        in_specs=[
            pl.BlockSpec((1, L, C_QK_WIDTH), tok),
            pl.BlockSpec((1, L, C_QK_WIDTH), tok),
            pl.BlockSpec((1, L, C_WIDTH), tok),
            pl.BlockSpec((1, L, C_WIDTH), tok),
            pl.BlockSpec((1, L, LANES), tok),
            pl.BlockSpec((1, SUBLANES, L), lambda b, c: (b, 0, c)),
            pl.BlockSpec((1, C_WIDTH), lambda b, c: (0, 0)),
        ],
        out_specs=pl.BlockSpec((1, L, C_WIDTH), tok),
        out_shape=jax.ShapeDtypeStruct((bn, s, C_WIDTH), BF16),
        scratch_shapes=[
            pltpu.VMEM((C_HEADS, C_DQK, C_DV), F32),
            pltpu.VMEM((C_HEADS, 1, C_DQK), F32),
            pltpu.VMEM((C_HEADS, 1, LANES), F32),
        ],
        compiler_params=_params(("arbitrary", "arbitrary")),
    )(q, k, v, og, gates, gates_t, head_g)


def _rope_tables(s):
    half = A_DH // 2
    inv = ROPE_THETA ** (-jnp.arange(half, dtype=F32) / half)
    ang = jnp.arange(s).astype(F32)[:, None] * inv[None, :]
    cos, sin = jnp.cos(ang), jnp.sin(ang)
    reps = LANES // half
    return jnp.tile(cos, (1, reps)), jnp.tile(sin, (1, reps))


def _mixer_ab(x, w_in, sgu_ln_g, sgu_ln_b, sgu_w, sgu_b, w_out, ln_g, ln_b):
    bn, s, d = x.shape
    cos, sin = _rope_tables(s)
    q, k, v, u, vn = _proj_ab(x, w_in.astype(BF16), cos, sin,
                              sgu_ln_g.reshape(1, B_WIDTH), sgu_ln_b.reshape(1, B_WIDTH))
    a = _moba(q, k, v)
    bs_full = jnp.repeat(sgu_b.T, B_DG, axis=1)
    bg = _sgu(u, vn, sgu_w, bs_full)
    out = _out_ln(x.reshape(bn * s, d), [a.reshape(bn * s, A_WIDTH), bg.reshape(bn * s, B_WIDTH)],
                  w_out.astype(BF16), ln_g.reshape(1, d), ln_b.reshape(1, d))
    return out.reshape(bn, s, d)


def _mixer_c(x, w_in, conv_w, conv_b, b_i, b_f, head_g, w_out, ln_g, ln_b):
    bn, s, d = x.shape
    qk_w = 2 * C_QK_WIDTH
    wqk = w_in[:, :qk_w].astype(BF16)
    wv = w_in[:, qk_w:qk_w + C_WIDTH].astype(BF16)
    wo = w_in[:, qk_w + C_WIDTH:qk_w + 2 * C_WIDTH].astype(BF16)
    w_if = w_in[:, qk_w + 2 * C_WIDTH:]
    wif = jnp.pad(w_if, ((0, 0), (0, LANES - 2 * C_HEADS))).astype(BF16)
    wift = w_if.T.astype(BF16)
    gate_bias = jnp.concatenate([b_i, b_f]).astype(F32)
    gb = jnp.pad(gate_bias, (0, LANES - 2 * C_HEADS)).reshape(1, LANES)
    gbt = gate_bias.reshape(2 * C_HEADS, 1)
    q, k, v, og, gates, gates_t = _proj_c(x, wqk, wv, wo, wif, wift, conv_w,
                                          conv_b.reshape(1, qk_w), gb, gbt)
    hg = _mlstm(q, k, v, og, gates, gates_t, head_g.reshape(1, C_WIDTH))
    out = _out_ln(x.reshape(bn * s, d), [hg.reshape(bn * s, C_WIDTH)], w_out.astype(BF16),
                  ln_g.reshape(1, d), ln_b.reshape(1, d))
    return out.reshape(bn, s, d)


def kernel(x, ln_g, ln_b, ffn_w_gu, ffn_w_down, ab_w_in, sgu_ln_g, sgu_ln_b, sgu_w, sgu_b, ab_w_out,
           c_w_in, c_conv_w, c_conv_b, c_b_i, c_b_f, c_head_g, c_w_out):
    bn, s, d = x.shape
    n = bn * s

    def ffn(xx, l, j, ln_idx):
        y = _ffn_ln(xx.reshape(n, d), ffn_w_gu[l, j].astype(BF16), ffn_w_down[l, j].astype(BF16),
                    ln_g[l, ln_idx].reshape(1, d), ln_b[l, ln_idx].reshape(1, d))
        return y.reshape(bn, s, d)

    for l in range(DEPTH):
        x = ffn(x, l, 0, 0)
        j = l // 2
        if l % 2 == 0:
            x = _mixer_ab(x, ab_w_in[j], sgu_ln_g[j], sgu_ln_b[j], sgu_w[j], sgu_b[j], ab_w_out[j],
                          ln_g[l, 1], ln_b[l, 1])
        else:
            x = _mixer_c(x, c_w_in[j], c_conv_w[j], c_conv_b[j], c_b_i[j], c_b_f[j], c_head_g[j],
                         c_w_out[j], ln_g[l, 1], ln_b[l, 1])
        x = ffn(x, l, 1, 2)
    return x
```

```python
import functools

import jax
import jax.numpy as jnp
from jax import lax
from jax.experimental import pallas as pl
from jax.experimental.pallas import tpu as pltpu

D_MODEL = 1024
DEPTH = 2
D_FF = 2816
ALPHA = (2 * DEPTH) ** 0.25
LN_EPS = 1e-5
NEG = -1e30

A_HEADS = 8
A_DH = 64
A_WIDTH = A_HEADS * A_DH
MOBA_BLOCK = 256
MOBA_TOPK = 3
ROPE_THETA = 10000.0

B_GROUPS = 8
B_DG = 64
B_WIDTH = B_GROUPS * B_DG
SGU_CHUNK = 128

C_HEADS = 4
C_DQK = 128
C_DV = 256
C_QK_WIDTH = C_HEADS * C_DQK
C_WIDTH = C_HEADS * C_DV
CONV_W = 4

LANES = 128
SUBLANES = 8
VMEM_LIMIT = 56 * 1024 * 1024

TOKEN_TILE = 512
FF_CHUNK = 256
MLSTM_L = 256
MOBA_KEY_BLOCKS = 2
LOG2E = 1.4426950408889634

BF16 = jnp.bfloat16
F32 = jnp.float32


def _params(sem):
    return pltpu.CompilerParams(dimension_semantics=sem, vmem_limit_bytes=VMEM_LIMIT)


def _dot(a, b):
    return jnp.dot(a, b, preferred_element_type=F32)


def _dot_nt(a, b):
    return lax.dot_general(a, b, (((1,), (1,)), ((), ())), preferred_element_type=F32)


def _dot_tn(a, b):
    return lax.dot_general(a, b, (((0,), (0,)), ((), ())), preferred_element_type=F32)


def _layer_norm(z, g, b):
    mu = jnp.mean(z, axis=-1, keepdims=True)
    zc = z - mu
    var = jnp.mean(zc * zc, axis=-1, keepdims=True)
    y = zc * lax.rsqrt(var + LN_EPS) * g
    return y if b is None else y + b


def _ffn_ln_kernel(x_ref, wgu_ref, wd_ref, g_ref, b_ref, o_ref, acc_ref):
    x = x_ref[...]
    xb = x.astype(BF16)
    for c in range(D_FF // FF_CHUNK):
        lo = c * FF_CHUNK
        gate = _dot(xb, wgu_ref[:, lo:lo + FF_CHUNK])
        up = _dot(xb, wgu_ref[:, D_FF + lo:D_FF + lo + FF_CHUNK])
        h = (jax.nn.silu(gate) * up).astype(BF16)
        part = _dot(h, wd_ref[lo:lo + FF_CHUNK, :])
        if c == 0:
            acc_ref[...] = part
        else:
            acc_ref[...] += part
    z = ALPHA * x + 0.5 * acc_ref[...]
    o_ref[...] = _layer_norm(z, g_ref[...], b_ref[...])


def _ffn_ln(x2, w_gu, w_down, g, b):
    n = x2.shape[0]
    tm = TOKEN_TILE
    const = lambda i: (0, 0)
    return pl.pallas_call(
        _ffn_ln_kernel,
        grid=(n // tm,),
        in_specs=[
            pl.BlockSpec((tm, D_MODEL), lambda i: (i, 0)),
            pl.BlockSpec((D_MODEL, 2 * D_FF), const),
            pl.BlockSpec((D_FF, D_MODEL), const),
            pl.BlockSpec((1, D_MODEL), const),
            pl.BlockSpec((1, D_MODEL), const),
        ],
        out_specs=pl.BlockSpec((tm, D_MODEL), lambda i: (i, 0)),
        out_shape=jax.ShapeDtypeStruct((n, D_MODEL), F32),
        scratch_shapes=[pltpu.VMEM((tm, D_MODEL), F32)],
        compiler_params=_params(("arbitrary",)),
        name="ffn_ln",
    )(x2, w_gu, w_down, g, b)


def _swap_halves_lanes(t, first_half):
    fwd = pltpu.roll(t, A_DH // 2, axis=1)
    bwd = pltpu.roll(t, LANES - A_DH // 2, axis=1)
    return jnp.where(first_half, bwd, fwd)


def _swap_halves_rows(t):
    half = A_DH // 2
    return jnp.concatenate([t[half:2 * half], t[:half], t[3 * half:], t[2 * half:3 * half]], axis=0)


def _proj_ab_kernel(x_ref, w_ref, wqvt_ref, cos_ref, sin_ref, cost_ref, sint_ref, lng_ref, lnb_ref,
                    qt_ref, k_ref, vt_ref, u_ref, vn_ref):
    xb = x_ref[0].astype(BF16)
    cos = cos_ref[...]
    sin = sin_ref[...]
    cos_t = cost_ref[...]
    sin_t = sint_ref[...]
    lane = lax.broadcasted_iota(jnp.int32, cos.shape, 1)
    first_half = (lane % A_DH) < (A_DH // 2)
    q_t = _dot_nt(wqvt_ref[:A_WIDTH, :], xb)
    for grp in range(A_WIDTH // LANES):
        lo = grp * LANES
        qg = q_t[lo:lo + LANES]
        qg = qg * cos_t + _swap_halves_rows(qg) * sin_t
        qt_ref[0, lo:lo + LANES, :] = (qg * (A_DH ** -0.5 * LOG2E)).astype(BF16)
        k = _dot(xb, w_ref[:, lo:lo + LANES])
        k = k * cos + _swap_halves_lanes(k, first_half) * sin
        k_ref[0, :, lo:lo + LANES] = k.astype(BF16)
    vt_ref[0] = _dot_nt(wqvt_ref[A_WIDTH:, :], xb).astype(BF16)
    ub = _dot(xb, w_ref[:, A_WIDTH:A_WIDTH + B_WIDTH])
    u_ref[0] = jax.nn.gelu(ub).astype(BF16)
    vb = _dot(xb, w_ref[:, A_WIDTH + B_WIDTH:])
    vn_ref[0] = _layer_norm(jax.nn.gelu(vb), lng_ref[...], lnb_ref[...]).astype(BF16)


def _proj_ab(x, w_kuv, w_qv_t, cos, sin, cos_t, sin_t, lng, lnb):
    bn, s, _ = x.shape
    tm = TOKEN_TILE
    const = lambda b, t: (0, 0)
    tok = lambda b, t: (b, t, 0)
    tok_t = lambda b, t: (b, 0, t)
    out = jax.ShapeDtypeStruct((bn, s, A_WIDTH), BF16)
    out_t = jax.ShapeDtypeStruct((bn, A_WIDTH, s), BF16)
    return pl.pallas_call(
        _proj_ab_kernel,
        grid=(bn, s // tm),
        in_specs=[
            pl.BlockSpec((1, tm, D_MODEL), tok),
            pl.BlockSpec(w_kuv.shape, const),
            pl.BlockSpec(w_qv_t.shape, const),
            pl.BlockSpec((tm, LANES), lambda b, t: (t, 0)),
            pl.BlockSpec((tm, LANES), lambda b, t: (t, 0)),
            pl.BlockSpec((LANES, tm), lambda b, t: (0, t)),
            pl.BlockSpec((LANES, tm), lambda b, t: (0, t)),
            pl.BlockSpec((1, B_WIDTH), const),
            pl.BlockSpec((1, B_WIDTH), const),
        ],
        out_specs=[pl.BlockSpec((1, A_WIDTH, tm), tok_t), pl.BlockSpec((1, tm, A_WIDTH), tok),
                   pl.BlockSpec((1, A_WIDTH, tm), tok_t), pl.BlockSpec((1, tm, B_WIDTH), tok),
                   pl.BlockSpec((1, tm, B_WIDTH), tok)],
        out_shape=[out_t, out, out_t, out, out],
        compiler_params=_params(("arbitrary", "arbitrary")),
        name="proj_ab",
    )(x, w_kuv, w_qv_t, cos, sin, cos_t, sin_t, lng, lnb)


def _moba_kernel(qt_ref, k_ref, vt_ref, o_ref, kmean_ref, sa_ref, sb_ref):
    i = pl.program_id(2)
    blk = MOBA_BLOCK
    nb = k_ref.shape[1] // blk
    lane_sq = lax.broadcasted_iota(jnp.int32, (LANES, LANES), 1)
    feat_row = lax.broadcasted_iota(jnp.int32, (LANES, blk), 0)

    @pl.when(i == 0)
    def _():
        kmean_ref[...] = jnp.zeros_like(kmean_ref)
        for j in range(nb):
            kj = k_ref[0, j * blk:(j + 1) * blk, :].astype(F32)
            kmean_ref[j:j + 1, :] = jnp.mean(kj, axis=0, keepdims=True)

    q_pair = qt_ref[0]
    own = pl.multiple_of(i * blk, blk)
    k_own = k_ref[0, pl.ds(own, blk), :]
    key_idx = lax.broadcasted_iota(jnp.int32, (blk, blk), 0)
    qry_idx = lax.broadcasted_iota(jnp.int32, (blk, blk), 1)
    causal = key_idx <= qry_idx
    kmean = kmean_ref[...]

    rhs, carry = [], []
    for h in range(2):
        head_rows = (feat_row < A_DH) if h == 0 else (feat_row >= A_DH)
        head_lanes = (lane_sq < A_DH) if h == 0 else (lane_sq >= A_DH)
        qh = jnp.where(head_rows, q_pair, jnp.zeros_like(q_pair))
        gate = _dot(jnp.where(head_lanes, kmean, 0.0).astype(BF16), qh)
        gate = jnp.where(feat_row < i, gate, NEG)
        chosen = jnp.zeros((LANES, blk), jnp.bool_)
        for _ in range(MOBA_TOPK):
            best = jnp.max(gate, axis=0, keepdims=True)
            idx = jnp.min(jnp.where(gate == best, feat_row, LANES), axis=0, keepdims=True)
            pick = feat_row == idx
            chosen = chosen | (pick & (idx < i))
            gate = jnp.where(pick, -jnp.inf, gate)
        bias = jnp.where(chosen, 0.0, NEG).astype(BF16)
        rhs.append(jnp.concatenate([qh, bias], axis=0))

        s = jnp.where(causal, _dot(k_own, qh), NEG)
        m0 = jnp.max(s, axis=0, keepdims=True)
        p = jnp.exp2(s - m0)
        vh = vt_ref[0, h * A_DH:(h + 1) * A_DH, pl.ds(own, blk)]
        carry += [m0, jnp.sum(p, axis=0, keepdims=True), _dot(vh, p.astype(BF16))]
    rhs_both = jnp.concatenate(rhs, axis=1)

    span = MOBA_KEY_BLOCKS * blk
    n_spans = nb // MOBA_KEY_BLOCKS
    span_block = lax.broadcasted_iota(jnp.int32, (span, LANES), 0) // blk
    span_lane = lax.broadcasted_iota(jnp.int32, (span, LANES), 1)

    def scores(c):
        c = jnp.minimum(c, n_spans - 1)
        start = pl.multiple_of(c * span, span)
        onehot = (span_lane == span_block + c * MOBA_KEY_BLOCKS).astype(BF16)
        lhs = jnp.concatenate([k_ref[0, pl.ds(start, span), :], onehot], axis=1)
        return _dot(lhs, rhs_both)

    def absorb(s_both, c, carry):
        start = pl.multiple_of(c * span, span)
        out = []
        for h in range(2):
            m_old, l_old, a_old = carry[3 * h], carry[3 * h + 1], carry[3 * h + 2]
            s = s_both[:, h * blk:(h + 1) * blk]
            m_new = jnp.maximum(m_old, jnp.max(s, axis=0, keepdims=True))
            alpha = jnp.exp2(m_old - m_new)
            p = jnp.exp2(s - m_new)
            vh = vt_ref[0, h * A_DH:(h + 1) * A_DH, pl.ds(start, span)]
            out += [m_new,
                    alpha * l_old + jnp.sum(p, axis=0, keepdims=True),
                    alpha * a_old + _dot(vh, p.astype(BF16))]
        return out

    def body(c2, carry):
        c = 2 * c2
        sb_ref[...] = scores(c + 1)
        carry = absorb(sa_ref[...], c, list(carry))
        sa_ref[...] = scores(c + 2)
        return tuple(absorb(sb_ref[...], c + 1, carry))

    sa_ref[...] = scores(0)
    past_spans = (i + MOBA_KEY_BLOCKS - 1) // MOBA_KEY_BLOCKS
    carry = lax.fori_loop(0, (past_spans + 1) // 2, body, tuple(carry))
    out_t = jnp.concatenate([carry[2] / carry[1], carry[5] / carry[4]], axis=0)
    o_ref[0] = out_t.T.astype(BF16)


def _moba(q_t, k, v_t):
    bn, s, _ = k.shape
    blk = MOBA_BLOCK
    return pl.pallas_call(
        _moba_kernel,
        grid=(bn, A_WIDTH // LANES, s // blk),
        in_specs=[
            pl.BlockSpec((1, LANES, blk), lambda b, p, i: (b, p, i)),
            pl.BlockSpec((1, s, LANES), lambda b, p, i: (b, 0, p)),
            pl.BlockSpec((1, LANES, s), lambda b, p, i: (b, p, 0)),
        ],
        out_specs=pl.BlockSpec((1, blk, LANES), lambda b, p, i: (b, i, p)),
        out_shape=jax.ShapeDtypeStruct((bn, s, A_WIDTH), BF16),
        scratch_shapes=[pltpu.VMEM((LANES, LANES), F32),
                        pltpu.VMEM((MOBA_KEY_BLOCKS * blk, 2 * blk), F32),
                        pltpu.VMEM((MOBA_KEY_BLOCKS * blk, 2 * blk), F32)],
        compiler_params=_params(("arbitrary", "arbitrary", "arbitrary")),
        name="moba",
    )(q_t, k, v_t)


def _sgu_kernel(u_ref, vn_ref, w_ref, bs_ref, o_ref):
    tm = u_ref.shape[1]
    t_row = lax.broadcasted_iota(jnp.int32, (SGU_CHUNK, SGU_CHUNK), 0)
    t_col = lax.broadcasted_iota(jnp.int32, (SGU_CHUNK, SGU_CHUNK), 1)
    causal = t_col <= t_row
    lane = lax.broadcasted_iota(jnp.int32, (SGU_CHUNK, LANES), 1)
    lo_mask = lane < B_DG
    for pair in range(B_WIDTH // LANES):
        w_lo = jnp.where(causal, w_ref[2 * pair], 0.0).astype(BF16)
        w_hi = jnp.where(causal, w_ref[2 * pair + 1], 0.0).astype(BF16)
        cols = slice(pair * LANES, (pair + 1) * LANES)
        bias = bs_ref[:, cols]
        for c in range(tm // SGU_CHUNK):
            rows = slice(c * SGU_CHUNK, (c + 1) * SGU_CHUNK)
            vp = vn_ref[0, rows, cols]
            zero = jnp.zeros_like(vp)
            mixed = _dot(w_lo, jnp.where(lo_mask, vp, zero)) + _dot(w_hi, jnp.where(lo_mask, zero, vp))
            o_ref[0, rows, cols] = (u_ref[0, rows, cols].astype(F32) * (mixed + bias)).astype(BF16)


def _sgu(u, vn, w_s, bs_full):
    bn, s, _ = u.shape
    tm = TOKEN_TILE
    tok = lambda b, t: (b, t, 0)
    return pl.pallas_call(
        _sgu_kernel,
        grid=(bn, s // tm),
        in_specs=[
            pl.BlockSpec((1, tm, B_WIDTH), tok),
            pl.BlockSpec((1, tm, B_WIDTH), tok),
            pl.BlockSpec(w_s.shape, lambda b, t: (0, 0, 0)),
            pl.BlockSpec(bs_full.shape, lambda b, t: (0, 0)),
        ],
        out_specs=pl.BlockSpec((1, tm, B_WIDTH), tok),
        out_shape=jax.ShapeDtypeStruct((bn, s, B_WIDTH), BF16),
        compiler_params=_params(("arbitrary", "arbitrary")),
        name="sgu",
    )(u, vn, w_s, bs_full)


def _out_ln_kernel(n_parts, *refs):
    x_ref = refs[0]
    part_refs = refs[1:1 + n_parts]
    w_ref, g_ref, b_ref, o_ref = refs[1 + n_parts:]
    y = None
    lo = 0
    for p_ref in part_refs:
        width = p_ref.shape[-1]
        term = _dot(p_ref[...], w_ref[lo:lo + width, :])
        y = term if y is None else y + term
        lo += width
    o_ref[...] = _layer_norm(ALPHA * x_ref[...] + y, g_ref[...], b_ref[...])


def _out_ln(x2, parts, w_out, g, b):
    n = x2.shape[0]
    tm = TOKEN_TILE
    const = lambda i: (0, 0)
    tok = lambda i: (i, 0)
    return pl.pallas_call(
        functools.partial(_out_ln_kernel, len(parts)),
        grid=(n // tm,),
        in_specs=[pl.BlockSpec((tm, D_MODEL), tok)]
        + [pl.BlockSpec((tm, p.shape[-1]), tok) for p in parts]
        + [pl.BlockSpec(w_out.shape, const), pl.BlockSpec((1, D_MODEL), const),
           pl.BlockSpec((1, D_MODEL), const)],
        out_specs=pl.BlockSpec((tm, D_MODEL), tok),
        out_shape=jax.ShapeDtypeStruct((n, D_MODEL), F32),
        compiler_params=_params(("arbitrary",)),
        name="out_ln",
    )(x2, *parts, w_out, g, b)


def _proj_c_kernel(x_ref, wqk_ref, wv_ref, wo_ref, wif_ref, wift_ref, cw_ref, cb_ref, gb_ref, gbt_ref,
                   q_ref, k_ref, v_ref, og_ref, gates_ref, gatest_ref, tail_ref):
    t = pl.program_id(1)
    tm = x_ref.shape[1]
    xb = x_ref[0].astype(BF16)

    @pl.when(t == 0)
    def _():
        tail_ref[...] = jnp.zeros_like(tail_ref)

    qk = _dot(xb, wqk_ref[...])
    ext = jnp.concatenate([tail_ref[...], qk], axis=0)
    tail_ref[...] = qk[tm - SUBLANES:, :]
    conv = cb_ref[...] + cw_ref[CONV_W - 1:CONV_W, :] * qk
    for d in range(1, CONV_W):
        conv = conv + cw_ref[CONV_W - 1 - d:CONV_W - d, :] * ext[SUBLANES - d:SUBLANES - d + tm, :]
    act = jax.nn.silu(conv)
    q_ref[0] = act[:, :C_QK_WIDTH].astype(BF16)
    k_ref[0] = (act[:, C_QK_WIDTH:] * (C_DQK ** -0.5)).astype(BF16)
    v_ref[0] = _dot(xb, wv_ref[...]).astype(BF16)
    og_ref[0] = jax.nn.sigmoid(_dot(xb, wo_ref[...])).astype(BF16)
    gates_ref[0] = _dot(xb, wif_ref[...]) + gb_ref[...]
    gatest_ref[0] = _dot_nt(wift_ref[...], xb) + gbt_ref[...]


def _proj_c(x, wqk, wv, wo, wif, wift, conv_w, conv_b, gb, gbt):
    bn, s, _ = x.shape
    tm = TOKEN_TILE
    tok = lambda b, t: (b, t, 0)
    const = lambda b, t: (0, 0)
    full = lambda a: pl.BlockSpec(a.shape, const)
    return pl.pallas_call(
        _proj_c_kernel,
        grid=(bn, s // tm),
        in_specs=[pl.BlockSpec((1, tm, D_MODEL), tok)]
        + [full(a) for a in (wqk, wv, wo, wif, wift, conv_w, conv_b, gb, gbt)],
        out_specs=[
            pl.BlockSpec((1, tm, C_QK_WIDTH), tok),
            pl.BlockSpec((1, tm, C_QK_WIDTH), tok),
            pl.BlockSpec((1, tm, C_WIDTH), tok),
            pl.BlockSpec((1, tm, C_WIDTH), tok),
            pl.BlockSpec((1, tm, LANES), tok),
            pl.BlockSpec((1, SUBLANES, tm), lambda b, t: (b, 0, t)),
        ],
        out_shape=[
            jax.ShapeDtypeStruct((bn, s, C_QK_WIDTH), BF16),
            jax.ShapeDtypeStruct((bn, s, C_QK_WIDTH), BF16),
            jax.ShapeDtypeStruct((bn, s, C_WIDTH), BF16),
            jax.ShapeDtypeStruct((bn, s, C_WIDTH), BF16),
            jax.ShapeDtypeStruct((bn, s, LANES), F32),
            jax.ShapeDtypeStruct((bn, SUBLANES, s), F32),
        ],
        scratch_shapes=[pltpu.VMEM((SUBLANES, 2 * C_QK_WIDTH), F32)],
        compiler_params=_params(("arbitrary", "arbitrary")),
        name="proj_c",
    )(x, wqk, wv, wo, wif, wift, conv_w, conv_b, gb, gbt)


def _split3(a):
    hi = a.astype(BF16)
    r1 = a - hi.astype(F32)
    mid = r1.astype(BF16)
    lo = (r1 - mid.astype(F32)).astype(BF16)
    return hi, mid, lo


def _mlstm_kernel(q_ref, k_ref, v_ref, og_ref, gates_ref, gatest_ref, hg_ref, o_ref,
                  c_ref, n_ref, m_ref):
    c_idx = pl.program_id(1)
    L = q_ref.shape[1]

    @pl.when(c_idx == 0)
    def _():
        c_ref[...] = jnp.zeros_like(c_ref)
        n_ref[...] = jnp.zeros_like(n_ref)
        m_ref[...] = jnp.zeros_like(m_ref)

    t_row = lax.broadcasted_iota(jnp.int32, (L, L), 0)
    t_col = lax.broadcasted_iota(jnp.int32, (L, L), 1)
    causal = t_col <= t_row
    tri = causal.astype(BF16)
    tri_t = (t_row <= t_col).astype(BF16)

    gates = gates_ref[0]
    gates_t = gatest_ref[0]
    b_cols = sum(_dot(tri, part) for part in _split3(jax.nn.log_sigmoid(gates)))
    b_rows = sum(_dot(part, tri_t) for part in _split3(jax.nn.log_sigmoid(gates_t)))

    for h in range(C_HEADS):
        i_col = gates[:, h:h + 1]
        i_row = gates_t[h:h + 1, :]
        b_col = b_cols[:, C_HEADS + h:C_HEADS + h + 1]
        b_row = b_rows[C_HEADS + h:C_HEADS + h + 1, :]
        b_end = b_col[L - 1:L, :]
        m_in = m_ref[h][:, 0:1]

        a_col = b_end - b_col + i_col
        a_max = jnp.max(a_col, axis=0, keepdims=True)
        w_end = jnp.exp(a_col - a_max)

        dmat = jnp.where(causal, b_col - b_row + i_row, -jnp.inf)
        g_col = b_col + m_in
        m_t = jnp.maximum(g_col, jnp.max(dmat, axis=-1, keepdims=True))
        pmat = jnp.exp(dmat - m_t)
        inter = jnp.exp(g_col - m_t)

        qh = q_ref[0, :, h * C_DQK:(h + 1) * C_DQK]
        kh = k_ref[0, :, h * C_DQK:(h + 1) * C_DQK]
        vh = v_ref[0, :, h * C_DV:(h + 1) * C_DV]
        c_in = c_ref[h]
        n_in = n_ref[h]

        sqk = _dot_nt(qh, kh) * pmat
        num = _dot(sqk.astype(BF16), vh) + inter * _dot(qh, c_in.astype(BF16))
        den = (jnp.sum(sqk, axis=-1, keepdims=True)
               + inter * jnp.sum(qh.astype(F32) * n_in, axis=-1, keepdims=True))
        hid = num / jnp.maximum(jnp.abs(den), jnp.exp(-m_t))
        hid = _layer_norm(hid, hg_ref[:, h * C_DV:(h + 1) * C_DV], None)
        gate_o = og_ref[0, :, h * C_DV:(h + 1) * C_DV].astype(F32)
        o_ref[0, :, h * C_DV:(h + 1) * C_DV] = (gate_o * hid).astype(BF16)

        m_new = jnp.maximum(b_end + m_in, a_max)
        decay = jnp.exp(b_end + m_in - m_new)
        inject = jnp.exp(a_max - m_new)
        kw = kh.astype(F32) * w_end
        c_ref[h] = decay * c_in + inject * _dot_tn(kw.astype(BF16), vh)
        n_ref[h] = decay * n_in + inject * jnp.sum(kw, axis=0, keepdims=True)
        m_ref[h] = jnp.broadcast_to(m_new, (1, LANES))


def _mlstm(q, k, v, og, gates, gates_t, head_g):
    bn, s, _ = q.shape
    L = MLSTM_L
    tok = lambda b, c: (b, c, 0)
    return pl.pallas_call(
        _mlstm_kernel,
        grid=(bn, s // L),
        in_specs=[
            pl.BlockSpec((1, L, C_QK_WIDTH), tok),
            pl.BlockSpec((1, L, C_QK_WIDTH), tok),
            pl.BlockSpec((1, L, C_WIDTH), tok),
            pl.BlockSpec((1, L, C_WIDTH), tok),
            pl.BlockSpec((1, L, LANES), tok),
            pl.BlockSpec((1, SUBLANES, L), lambda b, c: (b, 0, c)),
            pl.BlockSpec((1, C_WIDTH), lambda b, c: (0, 0)),
        ],
        out_specs=pl.BlockSpec((1, L, C_WIDTH), tok),
        out_shape=jax.ShapeDtypeStruct((bn, s, C_WIDTH), BF16),
        scratch_shapes=[
            pltpu.VMEM((C_HEADS, C_DQK, C_DV), F32),
            pltpu.VMEM((C_HEADS, 1, C_DQK), F32),
            pltpu.VMEM((C_HEADS, 1, LANES), F32),
        ],
        compiler_params=_params(("arbitrary", "arbitrary")),
        name="mlstm",
    )(q, k, v, og, gates, gates_t, head_g)


def _rope_tables(s):
    half = A_DH // 2
    inv = ROPE_THETA ** (-jnp.arange(half, dtype=F32) / half)
    ang = jnp.arange(s).astype(F32)[:, None] * inv[None, :]
    cos, sin = jnp.cos(ang), jnp.sin(ang)
    cos = jnp.tile(cos, (1, LANES // half))
    sin = jnp.tile(jnp.concatenate([-sin, sin], axis=1), (1, LANES // A_DH))
    return cos, sin, cos.T, sin.T


def _mixer_ab(x, w_in, sgu_ln_g, sgu_ln_b, sgu_w, sgu_b, w_out, ln_g, ln_b):
    bn, s, d = x.shape
    cos, sin, cos_t, sin_t = _rope_tables(s)
    w_kuv = jnp.concatenate([w_in[:, A_WIDTH:2 * A_WIDTH], w_in[:, 3 * A_WIDTH:]], axis=1).astype(BF16)
    w_qv_t = jnp.concatenate([w_in[:, :A_WIDTH], w_in[:, 2 * A_WIDTH:3 * A_WIDTH]], axis=1).T.astype(BF16)
    q_t, k, v_t, u, vn = _proj_ab(x, w_kuv, w_qv_t, cos, sin, cos_t, sin_t,
                                  sgu_ln_g.reshape(1, B_WIDTH), sgu_ln_b.reshape(1, B_WIDTH))
    a = _moba(q_t, k, v_t)
    bs_full = jnp.repeat(sgu_b.T, B_DG, axis=1)
    bg = _sgu(u, vn, sgu_w, bs_full)
    out = _out_ln(x.reshape(bn * s, d), [a.reshape(bn * s, A_WIDTH), bg.reshape(bn * s, B_WIDTH)],
                  w_out.astype(BF16), ln_g.reshape(1, d), ln_b.reshape(1, d))
    return out.reshape(bn, s, d)


def _mixer_c(x, w_in, conv_w, conv_b, b_i, b_f, head_g, w_out, ln_g, ln_b):
    bn, s, d = x.shape
    qk_w = 2 * C_QK_WIDTH
    wqk = w_in[:, :qk_w].astype(BF16)
    wv = w_in[:, qk_w:qk_w + C_WIDTH].astype(BF16)
    wo = w_in[:, qk_w + C_WIDTH:qk_w + 2 * C_WIDTH].astype(BF16)
    w_if = w_in[:, qk_w + 2 * C_WIDTH:]
    wif = jnp.pad(w_if, ((0, 0), (0, LANES - 2 * C_HEADS))).astype(BF16)
    wift = w_if.T.astype(BF16)
    gate_bias = jnp.concatenate([b_i, b_f]).astype(F32)
    gb = jnp.pad(gate_bias, (0, LANES - 2 * C_HEADS)).reshape(1, LANES)
    gbt = gate_bias.reshape(2 * C_HEADS, 1)
    q, k, v, og, gates, gates_t = _proj_c(x, wqk, wv, wo, wif, wift, conv_w,
                                          conv_b.reshape(1, qk_w), gb, gbt)
    hg = _mlstm(q, k, v, og, gates, gates_t, head_g.reshape(1, C_WIDTH))
    out = _out_ln(x.reshape(bn * s, d), [hg.reshape(bn * s, C_WIDTH)], w_out.astype(BF16),
                  ln_g.reshape(1, d), ln_b.reshape(1, d))
    return out.reshape(bn, s, d)


def kernel(x, ln_g, ln_b, ffn_w_gu, ffn_w_down, ab_w_in, sgu_ln_g, sgu_ln_b, sgu_w, sgu_b, ab_w_out,
           c_w_in, c_conv_w, c_conv_b, c_b_i, c_b_f, c_head_g, c_w_out):
    bn, s, d = x.shape
    n = bn * s

    def ffn(xx, l, j, ln_idx):
        y = _ffn_ln(xx.reshape(n, d), ffn_w_gu[l, j].astype(BF16), ffn_w_down[l, j].astype(BF16),
                    ln_g[l, ln_idx].reshape(1, d), ln_b[l, ln_idx].reshape(1, d))
        return y.reshape(bn, s, d)

    for l in range(DEPTH):
        x = ffn(x, l, 0, 0)
        j = l // 2
        if l % 2 == 0:
            x = _mixer_ab(x, ab_w_in[j], sgu_ln_g[j], sgu_ln_b[j], sgu_w[j], sgu_b[j], ab_w_out[j],
                          ln_g[l, 1], ln_b[l, 1])
        else:
            x = _mixer_c(x, c_w_in[j], c_conv_w[j], c_conv_b[j], c_b_i[j], c_b_f[j], c_head_g[j],
                         c_w_out[j], ln_g[l, 1], ln_b[l, 1])
        x = ffn(x, l, 1, 2)
    return x
```

```python
import functools

import jax
import jax.numpy as jnp
from jax import lax
from jax.experimental import pallas as pl
from jax.experimental.pallas import tpu as pltpu

D_MODEL = 1024
DEPTH = 2
D_FF = 2816
ALPHA = (2 * DEPTH) ** 0.25
LN_EPS = 1e-5
NEG = -1e30

A_HEADS = 8
A_DH = 64
A_WIDTH = A_HEADS * A_DH
MOBA_BLOCK = 256
MOBA_TOPK = 3
ROPE_THETA = 10000.0

B_GROUPS = 8
B_DG = 64
B_WIDTH = B_GROUPS * B_DG
SGU_CHUNK = 128

C_HEADS = 4
C_DQK = 128
C_DV = 256
C_QK_WIDTH = C_HEADS * C_DQK
C_WIDTH = C_HEADS * C_DV
CONV_W = 4

LANES = 128
SUBLANES = 8
VMEM_LIMIT = 56 * 1024 * 1024

TOKEN_TILE = 512
FF_CHUNK = 256
MLSTM_L = 256
MOBA_RING = 4
MOBA_SUM_ROWS = 16
MOBA_GATE_WIDTH = 1024
LOG2E = 1.4426950408889634

BF16 = jnp.bfloat16
F32 = jnp.float32


def _params(sem):
    return pltpu.CompilerParams(dimension_semantics=sem, vmem_limit_bytes=VMEM_LIMIT)


def _dot(a, b):
    return jnp.dot(a, b, preferred_element_type=F32)


def _dot_nt(a, b):
    return lax.dot_general(a, b, (((1,), (1,)), ((), ())), preferred_element_type=F32)


def _dot_tn(a, b):
    return lax.dot_general(a, b, (((0,), (0,)), ((), ())), preferred_element_type=F32)


def _layer_norm(z, g, b):
    mu = jnp.mean(z, axis=-1, keepdims=True)
    zc = z - mu
    var = jnp.mean(zc * zc, axis=-1, keepdims=True)
    y = zc * lax.rsqrt(var + LN_EPS) * g
    return y if b is None else y + b


def _ffn_ln_kernel(x_ref, wgu_ref, wd_ref, g_ref, b_ref, o_ref, acc_ref):
    x = x_ref[...]
    xb = x.astype(BF16)
    for c in range(D_FF // FF_CHUNK):
        lo = c * FF_CHUNK
        gate = _dot(xb, wgu_ref[:, lo:lo + FF_CHUNK])
        up = _dot(xb, wgu_ref[:, D_FF + lo:D_FF + lo + FF_CHUNK])
        h = (jax.nn.silu(gate) * up).astype(BF16)
        part = _dot(h, wd_ref[lo:lo + FF_CHUNK, :])
        if c == 0:
            acc_ref[...] = part
        else:
            acc_ref[...] += part
    z = ALPHA * x + 0.5 * acc_ref[...]
    o_ref[...] = _layer_norm(z, g_ref[...], b_ref[...])


def _ffn_ln(x2, w_gu, w_down, g, b):
    n = x2.shape[0]
    tm = TOKEN_TILE
    const = lambda i: (0, 0)
    return pl.pallas_call(
        _ffn_ln_kernel,
        grid=(n // tm,),
        in_specs=[
            pl.BlockSpec((tm, D_MODEL), lambda i: (i, 0)),
            pl.BlockSpec((D_MODEL, 2 * D_FF), const),
            pl.BlockSpec((D_FF, D_MODEL), const),
            pl.BlockSpec((1, D_MODEL), const),
            pl.BlockSpec((1, D_MODEL), const),
        ],
        out_specs=pl.BlockSpec((tm, D_MODEL), lambda i: (i, 0)),
        out_shape=jax.ShapeDtypeStruct((n, D_MODEL), F32),
        scratch_shapes=[pltpu.VMEM((tm, D_MODEL), F32)],
        compiler_params=_params(("arbitrary",)),
        name="ffn_ln",
    )(x2, w_gu, w_down, g, b)


def _swap_halves_lanes(t, first_half):
    fwd = pltpu.roll(t, A_DH // 2, axis=1)
    bwd = pltpu.roll(t, LANES - A_DH // 2, axis=1)
    return jnp.where(first_half, bwd, fwd)


def _swap_halves_rows(t):
    half = A_DH // 2
    return jnp.concatenate([t[half:2 * half], t[:half], t[3 * half:], t[2 * half:3 * half]], axis=0)


def _proj_ab_kernel(x_ref, w_ref, wqvt_ref, cos_ref, sin_ref, cost_ref, sint_ref, lng_ref, lnb_ref,
                    qt_ref, k_ref, vt_ref, u_ref, vn_ref):
    xb = x_ref[0].astype(BF16)
    cos = cos_ref[...]
    sin = sin_ref[...]
    cos_t = cost_ref[...]
    sin_t = sint_ref[...]
    lane = lax.broadcasted_iota(jnp.int32, cos.shape, 1)
    first_half = (lane % A_DH) < (A_DH // 2)
    q_t = _dot_nt(wqvt_ref[:A_WIDTH, :], xb)
    for grp in range(A_WIDTH // LANES):
        lo = grp * LANES
        qg = q_t[lo:lo + LANES]
        qg = qg * cos_t + _swap_halves_rows(qg) * sin_t
        qt_ref[0, lo:lo + LANES, :] = (qg * (A_DH ** -0.5 * LOG2E)).astype(BF16)
        k = _dot(xb, w_ref[:, lo:lo + LANES])
        k = k * cos + _swap_halves_lanes(k, first_half) * sin
        k_ref[0, :, lo:lo + LANES] = k.astype(BF16)
    vt_ref[0] = _dot_nt(wqvt_ref[A_WIDTH:, :], xb).astype(BF16)
    ub = _dot(xb, w_ref[:, A_WIDTH:A_WIDTH + B_WIDTH])
    u_ref[0] = jax.nn.gelu(ub).astype(BF16)
    vb = _dot(xb, w_ref[:, A_WIDTH + B_WIDTH:])
    vn_ref[0] = _layer_norm(jax.nn.gelu(vb), lng_ref[...], lnb_ref[...]).astype(BF16)


def _proj_ab(x, w_kuv, w_qv_t, cos, sin, cos_t, sin_t, lng, lnb):
    bn, s, _ = x.shape
    tm = TOKEN_TILE
    const = lambda b, t: (0, 0)
    tok = lambda b, t: (b, t, 0)
    tok_t = lambda b, t: (b, 0, t)
    out = jax.ShapeDtypeStruct((bn, s, A_WIDTH), BF16)
    out_t = jax.ShapeDtypeStruct((bn, A_WIDTH, s), BF16)
    return pl.pallas_call(
        _proj_ab_kernel,
        grid=(bn, s // tm),
        in_specs=[
            pl.BlockSpec((1, tm, D_MODEL), tok),
            pl.BlockSpec(w_kuv.shape, const),
            pl.BlockSpec(w_qv_t.shape, const),
            pl.BlockSpec((tm, LANES), lambda b, t: (t, 0)),
            pl.BlockSpec((tm, LANES), lambda b, t: (t, 0)),
            pl.BlockSpec((LANES, tm), lambda b, t: (0, t)),
            pl.BlockSpec((LANES, tm), lambda b, t: (0, t)),
            pl.BlockSpec((1, B_WIDTH), const),
            pl.BlockSpec((1, B_WIDTH), const),
        ],
        out_specs=[pl.BlockSpec((1, A_WIDTH, tm), tok_t), pl.BlockSpec((1, tm, A_WIDTH), tok),
                   pl.BlockSpec((1, A_WIDTH, tm), tok_t), pl.BlockSpec((1, tm, B_WIDTH), tok),
                   pl.BlockSpec((1, tm, B_WIDTH), tok)],
        out_shape=[out_t, out, out_t, out, out],
        compiler_params=_params(("arbitrary", "arbitrary")),
        name="proj_ab",
    )(x, w_kuv, w_qv_t, cos, sin, cos_t, sin_t, lng, lnb)


def _moba_gate_kernel(qt_ref, k_ref, avg_ref, bias_ref):
    blk = MOBA_BLOCK
    gate_rows = avg_ref.shape[0]
    s_len = qt_ref.shape[2]
    width = min(s_len, MOBA_GATE_WIDTH)
    kmean = _dot(avg_ref[...], k_ref[0])
    lane_g = lax.broadcasted_iota(jnp.int32, kmean.shape, 1)
    blk_row = lax.broadcasted_iota(jnp.int32, (gate_rows, width), 0)
    feat_row = lax.broadcasted_iota(jnp.int32, (LANES, width), 0)
    qry_lane = lax.broadcasted_iota(jnp.int32, (1, width), 1)
    for c in range(s_len // width):
        q_pair = qt_ref[0, :, c * width:(c + 1) * width]
        q_blk = (qry_lane + c * width) // blk
        for h in range(2):
            head_rows = (feat_row < A_DH) if h == 0 else (feat_row >= A_DH)
            head_lanes = (lane_g < A_DH) if h == 0 else (lane_g >= A_DH)
            qh = jnp.where(head_rows, q_pair, jnp.zeros_like(q_pair))
            gate = _dot(jnp.where(head_lanes, kmean, 0.0).astype(BF16), qh)
            gate = jnp.where(blk_row < q_blk, gate, NEG)
            chosen = jnp.zeros((gate_rows, width), jnp.bool_)
            for _ in range(MOBA_TOPK):
                best = jnp.max(gate, axis=0, keepdims=True)
                idx = jnp.min(jnp.where(gate == best, blk_row, LANES), axis=0, keepdims=True)
                pick = blk_row == idx
                chosen = chosen | (pick & (idx < q_blk))
                gate = jnp.where(pick, -jnp.inf, gate)
            bias_ref[0, h, :, c * width:(c + 1) * width] = jnp.where(chosen, 0.0, NEG).astype(BF16)


def _moba_gate(q_t, k):
    bn, s, _ = k.shape
    blk = MOBA_BLOCK
    nb = s // blk
    assert nb <= LANES
    gate_rows = -(-nb // 16) * 16
    block_avg = ((jnp.arange(gate_rows)[:, None] == jnp.arange(s)[None, :] // blk) / blk).astype(BF16)
    return pl.pallas_call(
        _moba_gate_kernel,
        grid=(bn, A_WIDTH // LANES),
        in_specs=[
            pl.BlockSpec((1, LANES, s), lambda b, p: (b, p, 0)),
            pl.BlockSpec((1, s, LANES), lambda b, p: (b, 0, p)),
            pl.BlockSpec((gate_rows, s), lambda b, p: (0, 0)),
        ],
        out_specs=pl.BlockSpec((1, 2, gate_rows, s), lambda b, p: (b, p, 0, 0)),
        out_shape=jax.ShapeDtypeStruct((bn, A_HEADS, gate_rows, s), BF16),
        compiler_params=_params(("arbitrary", "arbitrary")),
        name="moba_gate",
    )(q_t, k, block_avg)


def _moba_kernel(qt_ref, k_ref, vt_ref, bias_ref, e_ref, o_ref, *ring):
    i = pl.program_id(2)
    blk = MOBA_BLOCK
    nb = k_ref.shape[1] // blk
    gate_rows = bias_ref.shape[2]
    feat_row = lax.broadcasted_iota(jnp.int32, (LANES, blk), 0)

    q_pair = qt_ref[0]
    own = pl.multiple_of(i * blk, blk)
    k_own = k_ref[0, pl.ds(own, blk), :]
    key_idx = lax.broadcasted_iota(jnp.int32, (blk, blk), 0)
    qry_idx = lax.broadcasted_iota(jnp.int32, (blk, blk), 1)
    causal = key_idx <= qry_idx
    bias_pad = jnp.zeros((LANES - gate_rows, blk), BF16)
    ones_rows = jnp.ones((MOBA_SUM_ROWS, blk), BF16)

    def values(h, start):
        return jnp.concatenate([vt_ref[0, h * A_DH:(h + 1) * A_DH, pl.ds(start, blk)], ones_rows], axis=0)

    qh = [jnp.where(feat_row < A_DH, q_pair, jnp.zeros_like(q_pair)),
          jnp.where(feat_row >= A_DH, q_pair, jnp.zeros_like(q_pair))]
    rhs_both = jnp.concatenate([jnp.concatenate([qh[h], bias_ref[0, h], bias_pad], axis=0)
                                for h in range(2)], axis=1)
    s_own = _dot(k_own, jnp.concatenate(qh, axis=1))
    carry = []
    for h in range(2):
        s = jnp.where(causal, s_own[:, h * blk:(h + 1) * blk], NEG)
        m0 = jnp.max(s, axis=0, keepdims=True)
        p = jnp.exp2(s - m0)
        acc = _dot(values(h, own), p.astype(BF16))
        carry += [m0, acc]

    def key_start(j):
        return pl.multiple_of(jnp.minimum(j, nb - 1) * blk, blk)

    def scores(j):
        start = key_start(j)
        lhs = jnp.concatenate([k_ref[0, pl.ds(start, blk), :], e_ref[pl.ds(start, blk), :]], axis=1)
        return _dot(lhs, rhs_both)

    def absorb(s_both, j, carry):
        start = key_start(j)
        out = []
        for h in range(2):
            m_old, a_old = carry[2 * h], carry[2 * h + 1]
            s = s_both[:, h * blk:(h + 1) * blk]
            m_new = jnp.maximum(m_old, jnp.max(s, axis=0, keepdims=True))
            alpha = jnp.exp2(m_old - m_new)
            p = jnp.exp2(s - m_new)
            out += [m_new, alpha * a_old + _dot(values(h, start), p.astype(BF16))]
        return out

    depth = len(ring)
    ahead = depth - 2
    for r in range(ahead):
        ring[r][...] = scores(r)

    def body(it, carry):
        carry = list(carry)
        for r in range(depth):
            j = depth * it + r
            ring[(r + ahead) % depth][...] = scores(j + ahead)
            carry = absorb(ring[r][...], j, carry)
        return tuple(carry)

    carry = lax.fori_loop(0, (i + depth - 1) // depth, body, tuple(carry))
    out_t = jnp.concatenate([carry[2 * h + 1][:A_DH] / carry[2 * h + 1][A_DH:A_DH + 1] for h in range(2)],
                            axis=0)
    o_ref[0] = out_t.T.astype(BF16)


def _moba(q_t, k, v_t, bias):
    bn, s, _ = k.shape
    blk = MOBA_BLOCK
    nb = s // blk
    gate_rows = bias.shape[2]
    block_onehot = (jnp.arange(s)[:, None] // blk == jnp.arange(LANES)[None, :]).astype(BF16)
    return pl.pallas_call(
        _moba_kernel,
        grid=(bn, A_WIDTH // LANES, nb),
        in_specs=[
            pl.BlockSpec((1, LANES, blk), lambda b, p, i: (b, p, i)),
            pl.BlockSpec((1, s, LANES), lambda b, p, i: (b, 0, p)),
            pl.BlockSpec((1, LANES, s), lambda b, p, i: (b, p, 0)),
            pl.BlockSpec((1, 2, gate_rows, blk), lambda b, p, i: (b, p, 0, i)),
            pl.BlockSpec((s, LANES), lambda b, p, i: (0, 0)),
        ],
        out_specs=pl.BlockSpec((1, blk, LANES), lambda b, p, i: (b, i, p)),
        out_shape=jax.ShapeDtypeStruct((bn, s, A_WIDTH), BF16),
        scratch_shapes=[pltpu.VMEM((blk, 2 * blk), F32)] * MOBA_RING,
        compiler_params=_params(("arbitrary", "arbitrary", "arbitrary")),
        name="moba",
    )(q_t, k, v_t, bias, block_onehot)


def _sgu_kernel(u_ref, vn_ref, w_ref, bs_ref, o_ref):
    tm = u_ref.shape[1]
    t_row = lax.broadcasted_iota(jnp.int32, (SGU_CHUNK, SGU_CHUNK), 0)
    t_col = lax.broadcasted_iota(jnp.int32, (SGU_CHUNK, SGU_CHUNK), 1)
    causal = t_col <= t_row
    lane = lax.broadcasted_iota(jnp.int32, (SGU_CHUNK, LANES), 1)
    lo_mask = lane < B_DG
    for pair in range(B_WIDTH // LANES):
        w_lo = jnp.where(causal, w_ref[2 * pair], 0.0).astype(BF16)
        w_hi = jnp.where(causal, w_ref[2 * pair + 1], 0.0).astype(BF16)
        cols = slice(pair * LANES, (pair + 1) * LANES)
        bias = bs_ref[:, cols]
        for c in range(tm // SGU_CHUNK):
            rows = slice(c * SGU_CHUNK, (c + 1) * SGU_CHUNK)
            vp = vn_ref[0, rows, cols]
            zero = jnp.zeros_like(vp)
            mixed = _dot(w_lo, jnp.where(lo_mask, vp, zero)) + _dot(w_hi, jnp.where(lo_mask, zero, vp))
            o_ref[0, rows, cols] = (u_ref[0, rows, cols].astype(F32) * (mixed + bias)).astype(BF16)


def _sgu(u, vn, w_s, bs_full):
    bn, s, _ = u.shape
    tm = TOKEN_TILE
    tok = lambda b, t: (b, t, 0)
    return pl.pallas_call(
        _sgu_kernel,
        grid=(bn, s // tm),
        in_specs=[
            pl.BlockSpec((1, tm, B_WIDTH), tok),
            pl.BlockSpec((1, tm, B_WIDTH), tok),
            pl.BlockSpec(w_s.shape, lambda b, t: (0, 0, 0)),
            pl.BlockSpec(bs_full.shape, lambda b, t: (0, 0)),
        ],
        out_specs=pl.BlockSpec((1, tm, B_WIDTH), tok),
        out_shape=jax.ShapeDtypeStruct((bn, s, B_WIDTH), BF16),
        compiler_params=_params(("arbitrary", "arbitrary")),
        name="sgu",
    )(u, vn, w_s, bs_full)


def _out_ln_kernel(n_parts, *refs):
    x_ref = refs[0]
    part_refs = refs[1:1 + n_parts]
    w_ref, g_ref, b_ref, o_ref = refs[1 + n_parts:]
    y = None
    lo = 0
    for p_ref in part_refs:
        width = p_ref.shape[-1]
        term = _dot(p_ref[...], w_ref[lo:lo + width, :])
        y = term if y is None else y + term
        lo += width
    o_ref[...] = _layer_norm(ALPHA * x_ref[...] + y, g_ref[...], b_ref[...])


def _out_ln(x2, parts, w_out, g, b):
    n = x2.shape[0]
    tm = TOKEN_TILE
    const = lambda i: (0, 0)
    tok = lambda i: (i, 0)
    return pl.pallas_call(
        functools.partial(_out_ln_kernel, len(parts)),
        grid=(n // tm,),
        in_specs=[pl.BlockSpec((tm, D_MODEL), tok)]
        + [pl.BlockSpec((tm, p.shape[-1]), tok) for p in parts]
        + [pl.BlockSpec(w_out.shape, const), pl.BlockSpec((1, D_MODEL), const),
           pl.BlockSpec((1, D_MODEL), const)],
        out_specs=pl.BlockSpec((tm, D_MODEL), tok),
        out_shape=jax.ShapeDtypeStruct((n, D_MODEL), F32),
        compiler_params=_params(("arbitrary",)),
        name="out_ln",
    )(x2, *parts, w_out, g, b)


def _proj_c_kernel(x_ref, wqk_ref, wv_ref, wo_ref, wif_ref, wift_ref, cw_ref, cb_ref, gb_ref, gbt_ref,
                   q_ref, k_ref, v_ref, og_ref, gates_ref, gatest_ref, pre_ref):
    t = pl.program_id(1)
    tm = x_ref.shape[1]
    xb = x_ref[0].astype(BF16)

    @pl.when(t == 0)
    def _():
        pre_ref[:SUBLANES, :] = jnp.zeros((SUBLANES, pre_ref.shape[1]), F32)

    pre_ref[SUBLANES:, :] = _dot(xb, wqk_ref[...])
    conv = cb_ref[...] + cw_ref[CONV_W - 1:CONV_W, :] * pre_ref[SUBLANES:, :]
    for d in range(1, CONV_W):
        conv = conv + cw_ref[CONV_W - 1 - d:CONV_W - d, :] * pre_ref[SUBLANES - d:SUBLANES - d + tm, :]
    pre_ref[:SUBLANES, :] = pre_ref[tm:, :]
    act = jax.nn.silu(conv)
    q_ref[0] = act[:, :C_QK_WIDTH].astype(BF16)
    k_ref[0] = (act[:, C_QK_WIDTH:] * (C_DQK ** -0.5)).astype(BF16)
    v_ref[0] = _dot(xb, wv_ref[...]).astype(BF16)
    og_ref[0] = jax.nn.sigmoid(_dot(xb, wo_ref[...])).astype(BF16)
    gates_ref[0] = _dot(xb, wif_ref[...]) + gb_ref[...]
    gatest_ref[0] = _dot_nt(wift_ref[...], xb) + gbt_ref[...]


def _proj_c(x, wqk, wv, wo, wif, wift, conv_w, conv_b, gb, gbt):
    bn, s, _ = x.shape
    tm = TOKEN_TILE
    tok = lambda b, t: (b, t, 0)
    const = lambda b, t: (0, 0)
    full = lambda a: pl.BlockSpec(a.shape, const)
    return pl.pallas_call(
        _proj_c_kernel,
        grid=(bn, s // tm),
        in_specs=[pl.BlockSpec((1, tm, D_MODEL), tok)]
        + [full(a) for a in (wqk, wv, wo, wif, wift, conv_w, conv_b, gb, gbt)],
        out_specs=[
            pl.BlockSpec((1, tm, C_QK_WIDTH), tok),
            pl.BlockSpec((1, tm, C_QK_WIDTH), tok),
            pl.BlockSpec((1, tm, C_WIDTH), tok),
            pl.BlockSpec((1, tm, C_WIDTH), tok),
            pl.BlockSpec((1, tm, LANES), tok),
            pl.BlockSpec((1, SUBLANES, tm), lambda b, t: (b, 0, t)),
        ],
        out_shape=[
            jax.ShapeDtypeStruct((bn, s, C_QK_WIDTH), BF16),
            jax.ShapeDtypeStruct((bn, s, C_QK_WIDTH), BF16),
            jax.ShapeDtypeStruct((bn, s, C_WIDTH), BF16),
            jax.ShapeDtypeStruct((bn, s, C_WIDTH), BF16),
            jax.ShapeDtypeStruct((bn, s, LANES), F32),
            jax.ShapeDtypeStruct((bn, SUBLANES, s), F32),
        ],
        scratch_shapes=[pltpu.VMEM((SUBLANES + tm, 2 * C_QK_WIDTH), F32)],
        compiler_params=_params(("arbitrary", "arbitrary")),
        name="proj_c",
    )(x, wqk, wv, wo, wif, wift, conv_w, conv_b, gb, gbt)


def _split3(a):
    hi = a.astype(BF16)
    r1 = a - hi.astype(F32)
    mid = r1.astype(BF16)
    lo = (r1 - mid.astype(F32)).astype(BF16)
    return hi, mid, lo


def _mlstm_kernel(q_ref, k_ref, v_ref, og_ref, gates_ref, gatest_ref, hg_ref, o_ref,
                  c_ref, n_ref, m_ref):
    c_idx = pl.program_id(1)
    L = q_ref.shape[1]

    @pl.when(c_idx == 0)
    def _():
        c_ref[...] = jnp.zeros_like(c_ref)
        n_ref[...] = jnp.zeros_like(n_ref)
        m_ref[...] = jnp.zeros_like(m_ref)

    t_row = lax.broadcasted_iota(jnp.int32, (L, L), 0)
    t_col = lax.broadcasted_iota(jnp.int32, (L, L), 1)
    causal = t_col <= t_row
    tri = causal.astype(BF16)
    tri_t = (t_row <= t_col).astype(BF16)

    gates = gates_ref[0]
    gates_t = gatest_ref[0]
    b_cols = sum(_dot(tri, part) for part in _split3(jax.nn.log_sigmoid(gates)))
    b_rows = sum(_dot(part, tri_t) for part in _split3(jax.nn.log_sigmoid(gates_t)))

    for h in range(C_HEADS):
        i_col = gates[:, h:h + 1]
        i_row = gates_t[h:h + 1, :]
        b_col = b_cols[:, C_HEADS + h:C_HEADS + h + 1]
        b_row = b_rows[C_HEADS + h:C_HEADS + h + 1, :]
        b_end = b_col[L - 1:L, :]
        m_in = m_ref[h][:, 0:1]

        a_col = b_end - b_col + i_col
        a_max = jnp.max(a_col, axis=0, keepdims=True)
        w_end = jnp.exp(a_col - a_max)

        dmat = jnp.where(causal, b_col - b_row + i_row, -jnp.inf)
        g_col = b_col + m_in
        m_t = jnp.maximum(g_col, jnp.max(dmat, axis=-1, keepdims=True))
        pmat = jnp.exp(dmat - m_t)
        inter = jnp.exp(g_col - m_t)

        qh = q_ref[0, :, h * C_DQK:(h + 1) * C_DQK]
        kh = k_ref[0, :, h * C_DQK:(h + 1) * C_DQK]
        vh = v_ref[0, :, h * C_DV:(h + 1) * C_DV]
        c_in = c_ref[h]
        n_in = n_ref[h]

        sqk = _dot_nt(qh, kh) * pmat
        num = _dot(sqk.astype(BF16), vh) + inter * _dot(qh, c_in.astype(BF16))
        den = (jnp.sum(sqk, axis=-1, keepdims=True)
               + inter * jnp.sum(qh.astype(F32) * n_in, axis=-1, keepdims=True))
        hid = num / jnp.maximum(jnp.abs(den), jnp.exp(-m_t))
        hid = _layer_norm(hid, hg_ref[:, h * C_DV:(h + 1) * C_DV], None)
        gate_o = og_ref[0, :, h * C_DV:(h + 1) * C_DV].astype(F32)
        o_ref[0, :, h * C_DV:(h + 1) * C_DV] = (gate_o * hid).astype(BF16)

        m_new = jnp.maximum(b_end + m_in, a_max)
        decay = jnp.exp(b_end + m_in - m_new)
        inject = jnp.exp(a_max - m_new)
        kw = kh.astype(F32) * w_end
        c_ref[h] = decay * c_in + inject * _dot_tn(kw.astype(BF16), vh)
        n_ref[h] = decay * n_in + inject * jnp.sum(kw, axis=0, keepdims=True)
        m_ref[h] = jnp.broadcast_to(m_new, (1, LANES))


def _mlstm(q, k, v, og, gates, gates_t, head_g):
    bn, s, _ = q.shape
    L = MLSTM_L
    tok = lambda b, c: (b, c, 0)
    return pl.pallas_call(
        _mlstm_kernel,
        grid=(bn, s // L),
        in_specs=[
            pl.BlockSpec((1, L, C_QK_WIDTH), tok),
            pl.BlockSpec((1, L, C_QK_WIDTH), tok),
            pl.BlockSpec((1, L, C_WIDTH), tok),
            pl.BlockSpec((1, L, C_WIDTH), tok),
            pl.BlockSpec((1, L, LANES), tok),
            pl.BlockSpec((1, SUBLANES, L), lambda b, c: (b, 0, c)),
            pl.BlockSpec((1, C_WIDTH), lambda b, c: (0, 0)),
        ],
        out_specs=pl.BlockSpec((1, L, C_WIDTH), tok),
        out_shape=jax.ShapeDtypeStruct((bn, s, C_WIDTH), BF16),
        scratch_shapes=[
            pltpu.VMEM((C_HEADS, C_DQK, C_DV), F32),
            pltpu.VMEM((C_HEADS, 1, C_DQK), F32),
            pltpu.VMEM((C_HEADS, 1, LANES), F32),
        ],
        compiler_params=_params(("arbitrary", "arbitrary")),
        name="mlstm",
    )(q, k, v, og, gates, gates_t, head_g)


def _rope_tables(s):
    half = A_DH // 2
    inv = ROPE_THETA ** (-jnp.arange(half, dtype=F32) / half)
    ang = jnp.arange(s).astype(F32)[:, None] * inv[None, :]
    cos, sin = jnp.cos(ang), jnp.sin(ang)
    cos = jnp.tile(cos, (1, LANES // half))
    sin = jnp.tile(jnp.concatenate([-sin, sin], axis=1), (1, LANES // A_DH))
    return cos, sin, cos.T, sin.T


def _mixer_ab(x, w_in, sgu_ln_g, sgu_ln_b, sgu_w, sgu_b, w_out, ln_g, ln_b):
    bn, s, d = x.shape
    cos, sin, cos_t, sin_t = _rope_tables(s)
    w_kuv = jnp.concatenate([w_in[:, A_WIDTH:2 * A_WIDTH], w_in[:, 3 * A_WIDTH:]], axis=1).astype(BF16)
    w_qv_t = jnp.concatenate([w_in[:, :A_WIDTH], w_in[:, 2 * A_WIDTH:3 * A_WIDTH]], axis=1).T.astype(BF16)
    q_t, k, v_t, u, vn = _proj_ab(x, w_kuv, w_qv_t, cos, sin, cos_t, sin_t,
                                  sgu_ln_g.reshape(1, B_WIDTH), sgu_ln_b.reshape(1, B_WIDTH))
    a = _moba(q_t, k, v_t, _moba_gate(q_t, k))
    bs_full = jnp.repeat(sgu_b.T, B_DG, axis=1)
    bg = _sgu(u, vn, sgu_w, bs_full)
    out = _out_ln(x.reshape(bn * s, d), [a.reshape(bn * s, A_WIDTH), bg.reshape(bn * s, B_WIDTH)],
                  w_out.astype(BF16), ln_g.reshape(1, d), ln_b.reshape(1, d))
    return out.reshape(bn, s, d)


def _mixer_c(x, w_in, conv_w, conv_b, b_i, b_f, head_g, w_out, ln_g, ln_b):
    bn, s, d = x.shape
    qk_w = 2 * C_QK_WIDTH
    wqk = w_in[:, :qk_w].astype(BF16)
    wv = w_in[:, qk_w:qk_w + C_WIDTH].astype(BF16)
    wo = w_in[:, qk_w + C_WIDTH:qk_w + 2 * C_WIDTH].astype(BF16)
    w_if = w_in[:, qk_w + 2 * C_WIDTH:]
    wif = jnp.pad(w_if, ((0, 0), (0, LANES - 2 * C_HEADS))).astype(BF16)
    wift = w_if.T.astype(BF16)
    gate_bias = jnp.concatenate([b_i, b_f]).astype(F32)
    gb = jnp.pad(gate_bias, (0, LANES - 2 * C_HEADS)).reshape(1, LANES)
    gbt = gate_bias.reshape(2 * C_HEADS, 1)
    q, k, v, og, gates, gates_t = _proj_c(x, wqk, wv, wo, wif, wift, conv_w,
                                          conv_b.reshape(1, qk_w), gb, gbt)
    hg = _mlstm(q, k, v, og, gates, gates_t, head_g.reshape(1, C_WIDTH))
    out = _out_ln(x.reshape(bn * s, d), [hg.reshape(bn * s, C_WIDTH)], w_out.astype(BF16),
                  ln_g.reshape(1, d), ln_b.reshape(1, d))
    return out.reshape(bn, s, d)


def kernel(x, ln_g, ln_b, ffn_w_gu, ffn_w_down, ab_w_in, sgu_ln_g, sgu_ln_b, sgu_w, sgu_b, ab_w_out,
           c_w_in, c_conv_w, c_conv_b, c_b_i, c_b_f, c_head_g, c_w_out):
    bn, s, d = x.shape
    n = bn * s

    def ffn(xx, l, j, ln_idx):
        y = _ffn_ln(xx.reshape(n, d), ffn_w_gu[l, j].astype(BF16), ffn_w_down[l, j].astype(BF16),
                    ln_g[l, ln_idx].reshape(1, d), ln_b[l, ln_idx].reshape(1, d))
        return y.reshape(bn, s, d)

    for l in range(DEPTH):
        x = ffn(x, l, 0, 0)
        j = l // 2
        if l % 2 == 0:
            x = _mixer_ab(x, ab_w_in[j], sgu_ln_g[j], sgu_ln_b[j], sgu_w[j], sgu_b[j], ab_w_out[j],
                          ln_g[l, 1], ln_b[l, 1])
        else:
            x = _mixer_c(x, c_w_in[j], c_conv_w[j], c_conv_b[j], c_b_i[j], c_b_f[j], c_head_g[j],
                         c_w_out[j], ln_g[l, 1], ln_b[l, 1])
        x = ffn(x, l, 1, 2)
    return x
```

```python
import functools

import jax
import jax.numpy as jnp
from jax import lax
from jax.experimental import pallas as pl
from jax.experimental.pallas import tpu as pltpu

D_MODEL = 1024
DEPTH = 2
D_FF = 2816
ALPHA = (2 * DEPTH) ** 0.25
LN_EPS = 1e-5
NEG = -1e30

A_HEADS = 8
A_DH = 64
A_WIDTH = A_HEADS * A_DH
MOBA_BLOCK = 256
MOBA_TOPK = 3
ROPE_THETA = 10000.0

B_GROUPS = 8
B_DG = 64
B_WIDTH = B_GROUPS * B_DG
SGU_CHUNK = 128

C_HEADS = 4
C_DQK = 128
C_DV = 256
C_QK_WIDTH = C_HEADS * C_DQK
C_WIDTH = C_HEADS * C_DV
CONV_W = 4

LANES = 128
SUBLANES = 8
VMEM_LIMIT = 56 * 1024 * 1024

TOKEN_TILE = 512
FF_CHUNK = 256
PROJ_CHUNK = 256
MLSTM_L = 256
MOBA_RING = 4
MOBA_PAIRS = 4
MOBA_SUM_ROWS = 16
MOBA_GATE_WIDTH = 1024
LOG2E = 1.4426950408889634

BF16 = jnp.bfloat16
F32 = jnp.float32


def _params(sem):
    return pltpu.CompilerParams(dimension_semantics=sem, vmem_limit_bytes=VMEM_LIMIT)


def _dot(a, b):
    return jnp.dot(a, b, preferred_element_type=F32)


def _dot_nt(a, b):
    return lax.dot_general(a, b, (((1,), (1,)), ((), ())), preferred_element_type=F32)


def _layer_norm(z, g, b):
    mu = jnp.mean(z, axis=-1, keepdims=True)
    zc = z - mu
    var = jnp.mean(zc * zc, axis=-1, keepdims=True)
    y = zc * lax.rsqrt(var + LN_EPS) * g
    return y if b is None else y + b


def _resident(shape):
    return pl.BlockSpec(shape, lambda *_: (0,) * len(shape), pipeline_mode=pl.Buffered(1))


def _ffn_ln_value(x, wgu_ref, wd_ref, g_ref, b_ref, acc_ref):
    xb = x.astype(BF16)
    for c in range(D_FF // FF_CHUNK):
        lo = c * FF_CHUNK
        gate = _dot(xb, wgu_ref[:, lo:lo + FF_CHUNK])
        up = _dot(xb, wgu_ref[:, D_FF + lo:D_FF + lo + FF_CHUNK])
        h = (jax.nn.silu(gate) * up).astype(BF16)
        part = _dot(h, wd_ref[lo:lo + FF_CHUNK, :])
        if c == 0:
            acc_ref[...] = part
        else:
            acc_ref[...] += part
    z = ALPHA * x + 0.5 * acc_ref[...]
    return _layer_norm(z, g_ref[...], b_ref[...])


def _ffn_ln_kernel(x_ref, wgu_ref, wd_ref, g_ref, b_ref, o_ref, acc_ref):
    o_ref[...] = _ffn_ln_value(x_ref[...], wgu_ref, wd_ref, g_ref, b_ref, acc_ref)


def _ffn_ln(x2, w_gu, w_down, g, b):
    n = x2.shape[0]
    tm = TOKEN_TILE
    return pl.pallas_call(
        _ffn_ln_kernel,
        grid=(n // tm,),
        in_specs=[pl.BlockSpec((tm, D_MODEL), lambda i: (i, 0))]
        + [_resident(a.shape) for a in (w_gu, w_down, g, b)],
        out_specs=pl.BlockSpec((tm, D_MODEL), lambda i: (i, 0)),
        out_shape=jax.ShapeDtypeStruct((n, D_MODEL), F32),
        scratch_shapes=[pltpu.VMEM((tm, D_MODEL), F32)],
        compiler_params=_params(("arbitrary",)),
        name="ffn_ln",
    )(x2, w_gu, w_down, g, b)


def _swap_halves_lanes(t, first_half):
    fwd = pltpu.roll(t, A_DH // 2, axis=1)
    bwd = pltpu.roll(t, LANES - A_DH // 2, axis=1)
    return jnp.where(first_half, bwd, fwd)


def _swap_halves_rows(t):
    half = A_DH // 2
    return jnp.concatenate([t[half:2 * half], t[:half], t[3 * half:], t[2 * half:3 * half]], axis=0)


def _proj_ab_kernel(x_ref, w_ref, wqvt_ref, cos_ref, sin_ref, cost_ref, sint_ref, lng_ref, lnb_ref,
                    qt_ref, k_ref, vt_ref, u_ref, vn_ref):
    xb = x_ref[0].astype(BF16)
    cos = cos_ref[...]
    sin = sin_ref[...]
    cos_t = cost_ref[...]
    sin_t = sint_ref[...]
    lane = lax.broadcasted_iota(jnp.int32, cos.shape, 1)
    first_half = (lane % A_DH) < (A_DH // 2)
    q_t = _dot_nt(wqvt_ref[:A_WIDTH, :], xb)
    for grp in range(A_WIDTH // LANES):
        lo = grp * LANES
        qg = q_t[lo:lo + LANES]
        qg = qg * cos_t + _swap_halves_rows(qg) * sin_t
        qt_ref[0, lo:lo + LANES, :] = (qg * (A_DH ** -0.5 * LOG2E)).astype(BF16)
        k = _dot(xb, w_ref[:, lo:lo + LANES])
        k = k * cos + _swap_halves_lanes(k, first_half) * sin
        k_ref[0, :, lo:lo + LANES] = k.astype(BF16)
    vt_ref[0] = _dot_nt(wqvt_ref[A_WIDTH:, :], xb).astype(BF16)
    ub = _dot(xb, w_ref[:, A_WIDTH:A_WIDTH + B_WIDTH])
    u_ref[0] = jax.nn.gelu(ub).astype(BF16)
    vb = _dot(xb, w_ref[:, A_WIDTH + B_WIDTH:])
    vn_ref[0] = _layer_norm(jax.nn.gelu(vb), lng_ref[...], lnb_ref[...]).astype(BF16)


def _proj_ab(x, w_kuv, w_qv_t, cos, sin, cos_t, sin_t, lng, lnb):
    bn, s, _ = x.shape
    tm = TOKEN_TILE
    const = lambda b, t: (0, 0)
    tok = lambda b, t: (b, t, 0)
    tok_t = lambda b, t: (b, 0, t)
    out = jax.ShapeDtypeStruct((bn, s, A_WIDTH), BF16)
    out_t = jax.ShapeDtypeStruct((bn, A_WIDTH, s), BF16)
    return pl.pallas_call(
        _proj_ab_kernel,
        grid=(bn, s // tm),
        in_specs=[
            pl.BlockSpec((1, tm, D_MODEL), tok),
            _resident(w_kuv.shape),
            _resident(w_qv_t.shape),
            pl.BlockSpec((tm, LANES), lambda b, t: (t, 0)),
            pl.BlockSpec((tm, LANES), lambda b, t: (t, 0)),
            pl.BlockSpec((LANES, tm), lambda b, t: (0, t)),
            pl.BlockSpec((LANES, tm), lambda b, t: (0, t)),
            pl.BlockSpec((1, B_WIDTH), const),
            pl.BlockSpec((1, B_WIDTH), const),
        ],
        out_specs=[pl.BlockSpec((1, A_WIDTH, tm), tok_t), pl.BlockSpec((1, tm, A_WIDTH), tok),
                   pl.BlockSpec((1, A_WIDTH, tm), tok_t), pl.BlockSpec((1, tm, B_WIDTH), tok),
                   pl.BlockSpec((1, tm, B_WIDTH), tok)],
        out_shape=[out_t, out, out_t, out, out],
        compiler_params=_params(("arbitrary", "arbitrary")),
        name="proj_ab",
    )(x, w_kuv, w_qv_t, cos, sin, cos_t, sin_t, lng, lnb)


def _moba_gate_kernel(qt_ref, k_ref, avg_ref, bias_ref):
    blk = MOBA_BLOCK
    gate_rows = avg_ref.shape[0]
    s_len = qt_ref.shape[2]
    width = min(s_len, MOBA_GATE_WIDTH)
    kmean = _dot(avg_ref[...], k_ref[0])
    lane_g = lax.broadcasted_iota(jnp.int32, kmean.shape, 1)
    blk_row = lax.broadcasted_iota(jnp.int32, (gate_rows, width), 0)
    feat_row = lax.broadcasted_iota(jnp.int32, (LANES, width), 0)
    qry_lane = lax.broadcasted_iota(jnp.int32, (1, width), 1)
    for c in range(s_len // width):
        q_pair = qt_ref[0, :, c * width:(c + 1) * width]
        q_blk = (qry_lane + c * width) // blk
        for h in range(2):
            head_rows = (feat_row < A_DH) if h == 0 else (feat_row >= A_DH)
            head_lanes = (lane_g < A_DH) if h == 0 else (lane_g >= A_DH)
            qh = jnp.where(head_rows, q_pair, jnp.zeros_like(q_pair))
            gate = _dot(jnp.where(head_lanes, kmean, 0.0).astype(BF16), qh)
            gate = jnp.where(blk_row < q_blk, gate, NEG)
            chosen = jnp.zeros((gate_rows, width), jnp.bool_)
            for _ in range(MOBA_TOPK):
                best = jnp.max(gate, axis=0, keepdims=True)
                idx = jnp.min(jnp.where(gate == best, blk_row, LANES), axis=0, keepdims=True)
                pick = blk_row == idx
                chosen = chosen | (pick & (idx < q_blk))
                gate = jnp.where(pick, -jnp.inf, gate)
            bias_ref[0, h, :, c * width:(c + 1) * width] = jnp.where(chosen, 0.0, NEG).astype(BF16)


def _moba_gate(q_t, k):
    bn, s, _ = k.shape
    blk = MOBA_BLOCK
    nb = s // blk
    assert nb <= LANES
    gate_rows = -(-nb // 16) * 16
    block_avg = ((jnp.arange(gate_rows)[:, None] == jnp.arange(s)[None, :] // blk) / blk).astype(BF16)
    return pl.pallas_call(
        _moba_gate_kernel,
        grid=(bn, A_WIDTH // LANES),
        in_specs=[
            pl.BlockSpec((1, LANES, s), lambda b, p: (b, p, 0)),
            pl.BlockSpec((1, s, LANES), lambda b, p: (b, 0, p)),
            _resident((gate_rows, s)),
        ],
        out_specs=pl.BlockSpec((1, 2, gate_rows, s), lambda b, p: (b, p, 0, 0)),
        out_shape=jax.ShapeDtypeStruct((bn, A_HEADS, gate_rows, s), BF16),
        compiler_params=_params(("arbitrary", "arbitrary")),
        name="moba_gate",
    )(q_t, k, block_avg)


def _moba_kernel(qt_ref, k_ref, vt_ref, bias_ref, e_ref, o_ref, *ring):
    i = pl.program_id(2)
    blk = MOBA_BLOCK
    nb = k_ref.shape[1] // blk
    pairs = qt_ref.shape[1] // LANES
    depth = len(ring) // pairs
    ahead = depth - 2
    gate_rows = bias_ref.shape[2]
    feat_row = lax.broadcasted_iota(jnp.int32, (LANES, blk), 0)
    key_idx = lax.broadcasted_iota(jnp.int32, (blk, blk), 0)
    qry_idx = lax.broadcasted_iota(jnp.int32, (blk, blk), 1)
    causal = key_idx <= qry_idx
    bias_pad = jnp.zeros((LANES - gate_rows, blk), BF16)
    ones_rows = jnp.ones((MOBA_SUM_ROWS, blk), BF16)
    own = pl.multiple_of(i * blk, blk)

    def key_start(j):
        return pl.multiple_of(jnp.minimum(j, nb - 1) * blk, blk)

    def values(head, start):
        return jnp.concatenate([vt_ref[0, head * A_DH:(head + 1) * A_DH, pl.ds(start, blk)], ones_rows], axis=0)

    def absorb(pair, s_both, start, carry, mask=None):
        out = []
        for h in range(2):
            s = s_both[:, h * blk:(h + 1) * blk]
            if mask is not None:
                s = jnp.where(mask, s, NEG)
            m_new = jnp.max(s, axis=0, keepdims=True)
            pv = lambda p: _dot(values(2 * pair + h, start), p.astype(BF16))
            if carry is None:
                out += [m_new, pv(jnp.exp2(s - m_new))]
            else:
                m_old, a_old = carry[2 * h], carry[2 * h + 1]
                m_new = jnp.maximum(m_old, m_new)
                out += [m_new, jnp.exp2(m_old - m_new) * a_old + pv(jnp.exp2(s - m_new))]
        return out

    rhs, carry = [], []
    for pair in range(pairs):
        rows = slice(pair * LANES, (pair + 1) * LANES)
        q_pair = qt_ref[0, rows, :]
        qh = [jnp.where(feat_row < A_DH, q_pair, jnp.zeros_like(q_pair)),
              jnp.where(feat_row >= A_DH, q_pair, jnp.zeros_like(q_pair))]
        rhs.append(jnp.concatenate([jnp.concatenate([qh[h], bias_ref[0, 2 * pair + h], bias_pad], axis=0)
                                    for h in range(2)], axis=1))
        s_own = _dot(k_ref[0, pl.ds(own, blk), rows], jnp.concatenate(qh, axis=1))
        carry += absorb(pair, s_own, own, None, causal)

    def scores(pair, j):
        start = key_start(j)
        lhs = jnp.concatenate([k_ref[0, pl.ds(start, blk), pair * LANES:(pair + 1) * LANES],
                               e_ref[pl.ds(start, blk), :]], axis=1)
        return _dot(lhs, rhs[pair])

    for pair in range(pairs):
        for r in range(ahead):
            ring[pair * depth + r][...] = scores(pair, r)

    def body(it, carry):
        carry = list(carry)
        for r in range(depth):
            j = depth * it + r
            for pair in range(pairs):
                ring[pair * depth + (r + ahead) % depth][...] = scores(pair, j + ahead)
                carry[4 * pair:4 * pair + 4] = absorb(pair, ring[pair * depth + r][...], key_start(j),
                                                      carry[4 * pair:4 * pair + 4])
        return tuple(carry)

    carry = lax.fori_loop(0, (i + depth - 1) // depth, body, tuple(carry))
    out_t = jnp.concatenate([acc[:A_DH] / acc[A_DH:A_DH + 1] for acc in carry[1::2]], axis=0)
    o_ref[0] = out_t.T.astype(BF16)


def _moba(q_t, k, v_t, bias):
    bn, s, _ = k.shape
    blk = MOBA_BLOCK
    nb = s // blk
    gate_rows = bias.shape[2]
    width = MOBA_PAIRS * LANES
    block_onehot = (jnp.arange(s)[:, None] // blk == jnp.arange(LANES)[None, :]).astype(BF16)
    return pl.pallas_call(
        _moba_kernel,
        grid=(bn, A_WIDTH // width, nb),
        in_specs=[
            pl.BlockSpec((1, width, blk), lambda b, p, i: (b, p, i)),
            pl.BlockSpec((1, s, width), lambda b, p, i: (b, 0, p)),
            pl.BlockSpec((1, width, s), lambda b, p, i: (b, p, 0)),
            pl.BlockSpec((1, 2 * MOBA_PAIRS, gate_rows, blk), lambda b, p, i: (b, p, 0, i)),
            _resident((s, LANES)),
        ],
        out_specs=pl.BlockSpec((1, blk, width), lambda b, p, i: (b, i, p)),
        out_shape=jax.ShapeDtypeStruct((bn, s, A_WIDTH), BF16),
        scratch_shapes=[pltpu.VMEM((blk, 2 * blk), F32)] * (MOBA_PAIRS * MOBA_RING),
        compiler_params=_params(("arbitrary", "arbitrary", "arbitrary")),
        name="moba",
    )(q_t, k, v_t, bias, block_onehot)


def _sgu_kernel(u_ref, vn_ref, w_ref, bs_ref, o_ref):
    tm = u_ref.shape[1]
    t_row = lax.broadcasted_iota(jnp.int32, (SGU_CHUNK, SGU_CHUNK), 0)
    t_col = lax.broadcasted_iota(jnp.int32, (SGU_CHUNK, SGU_CHUNK), 1)
    causal = t_col <= t_row
    lane = lax.broadcasted_iota(jnp.int32, (SGU_CHUNK, LANES), 1)
    lo_mask = lane < B_DG
    for pair in range(B_WIDTH // LANES):
        w_lo = jnp.where(causal, w_ref[2 * pair], 0.0).astype(BF16)
        w_hi = jnp.where(causal, w_ref[2 * pair + 1], 0.0).astype(BF16)
        cols = slice(pair * LANES, (pair + 1) * LANES)
        bias = bs_ref[:, cols]
        for c in range(tm // SGU_CHUNK):
            rows = slice(c * SGU_CHUNK, (c + 1) * SGU_CHUNK)
            vp = vn_ref[0, rows, cols]
            zero = jnp.zeros_like(vp)
            mixed = _dot(w_lo, jnp.where(lo_mask, vp, zero)) + _dot(w_hi, jnp.where(lo_mask, zero, vp))
            o_ref[0, rows, cols] = (u_ref[0, rows, cols].astype(F32) * (mixed + bias)).astype(BF16)


def _sgu(u, vn, w_s, bs_full):
    bn, s, _ = u.shape
    tm = TOKEN_TILE
    tok = lambda b, t: (b, t, 0)
    return pl.pallas_call(
        _sgu_kernel,
        grid=(bn, s // tm),
        in_specs=[
            pl.BlockSpec((1, tm, B_WIDTH), tok),
            pl.BlockSpec((1, tm, B_WIDTH), tok),
            pl.BlockSpec(w_s.shape, lambda b, t: (0, 0, 0)),
            pl.BlockSpec(bs_full.shape, lambda b, t: (0, 0)),
        ],
        out_specs=pl.BlockSpec((1, tm, B_WIDTH), tok),
        out_shape=jax.ShapeDtypeStruct((bn, s, B_WIDTH), BF16),
        compiler_params=_params(("arbitrary", "arbitrary")),
        name="sgu",
    )(u, vn, w_s, bs_full)


def _out_ffn_kernel(n_parts, *refs):
    x_ref = refs[0]
    part_refs = refs[1:1 + n_parts]
    w_ref, g1_ref, b1_ref, wgu_ref, wd_ref, g2_ref, b2_ref, o_ref, acc_ref = refs[1 + n_parts:]
    y = None
    lo = 0
    for p_ref in part_refs:
        width = p_ref.shape[-1]
        term = _dot(p_ref[...], w_ref[lo:lo + width, :])
        y = term if y is None else y + term
        lo += width
    mid = _layer_norm(ALPHA * x_ref[...] + y, g1_ref[...], b1_ref[...])
    o_ref[...] = _ffn_ln_value(mid, wgu_ref, wd_ref, g2_ref, b2_ref, acc_ref)


def _out_ffn(x2, parts, w_out, g1, b1, w_gu, w_down, g2, b2):
    n = x2.shape[0]
    tm = TOKEN_TILE
    tok = lambda i: (i, 0)
    consts = (w_out, g1, b1, w_gu, w_down, g2, b2)
    return pl.pallas_call(
        functools.partial(_out_ffn_kernel, len(parts)),
        grid=(n // tm,),
        in_specs=[pl.BlockSpec((tm, D_MODEL), tok)]
        + [pl.BlockSpec((tm, p.shape[-1]), tok) for p in parts]
        + [_resident(a.shape) for a in consts],
        out_specs=pl.BlockSpec((tm, D_MODEL), tok),
        out_shape=jax.ShapeDtypeStruct((n, D_MODEL), F32),
        scratch_shapes=[pltpu.VMEM((tm, D_MODEL), F32)],
        compiler_params=_params(("arbitrary",)),
        name="out_ffn",
    )(x2, *parts, *consts)


def _proj_c_kernel(x_ref, wqk_ref, wvt_ref, wot_ref, wif_ref, wift_ref, cw_ref, cb_ref, gb_ref, gbt_ref,
                   q_ref, k_ref, vt_ref, ogt_ref, gates_ref, gatest_ref, *pre_refs):
    t = pl.program_id(1)
    tm = x_ref.shape[1]
    xb = x_ref[0].astype(BF16)
    width = pre_refs[0].shape[1]

    @pl.when(t == 0)
    def _():
        for pre_ref in pre_refs:
            pre_ref[:SUBLANES, :] = jnp.zeros((SUBLANES, width), F32)

    gates_ref[0] = _dot(xb, wif_ref[...]) + gb_ref[...]
    gatest_ref[0] = _dot_nt(wift_ref[...], xb) + gbt_ref[...]
    for c, pre_ref in enumerate(pre_refs):
        pre_ref[SUBLANES:, :] = _dot(xb, wqk_ref[:, c * width:(c + 1) * width])
    for c, pre_ref in enumerate(pre_refs):
        cols = slice(c * width, (c + 1) * width)
        vt_ref[0, cols, :] = _dot_nt(wvt_ref[cols, :], xb).astype(BF16)
        conv = cb_ref[:, cols] + cw_ref[CONV_W - 1:CONV_W, cols] * pre_ref[SUBLANES:, :]
        for d in range(1, CONV_W):
            conv = conv + cw_ref[CONV_W - 1 - d:CONV_W - d, cols] * pre_ref[SUBLANES - d:SUBLANES - d + tm, :]
        pre_ref[:SUBLANES, :] = pre_ref[tm:, :]
        act = jax.nn.silu(conv)
        if c * width < C_QK_WIDTH:
            q_ref[0, :, cols] = act.astype(BF16)
        else:
            k_cols = slice(c * width - C_QK_WIDTH, (c + 1) * width - C_QK_WIDTH)
            k_ref[0, :, k_cols] = (act * (C_DQK ** -0.5)).astype(BF16)
        ogt_ref[0, cols, :] = jax.nn.sigmoid(_dot_nt(wot_ref[cols, :], xb)).astype(BF16)


def _proj_c(x, wqk, wv_t, wo_t, wif, wift, conv_w, conv_b, gb, gbt):
    bn, s, _ = x.shape
    tm = TOKEN_TILE
    tok = lambda b, t: (b, t, 0)
    return pl.pallas_call(
        _proj_c_kernel,
        grid=(bn, s // tm),
        in_specs=[pl.BlockSpec((1, tm, D_MODEL), tok)]
        + [_resident(a.shape) for a in (wqk, wv_t, wo_t, wif, wift, conv_w, conv_b, gb, gbt)],
        out_specs=[
            pl.BlockSpec((1, tm, C_QK_WIDTH), tok),
            pl.BlockSpec((1, tm, C_QK_WIDTH), tok),
            pl.BlockSpec((1, C_WIDTH, tm), lambda b, t: (b, 0, t)),
            pl.BlockSpec((1, C_WIDTH, tm), lambda b, t: (b, 0, t)),
            pl.BlockSpec((1, tm, LANES), tok),
            pl.BlockSpec((1, SUBLANES, tm), lambda b, t: (b, 0, t)),
        ],
        out_shape=[
            jax.ShapeDtypeStruct((bn, s, C_QK_WIDTH), BF16),
            jax.ShapeDtypeStruct((bn, s, C_QK_WIDTH), BF16),
            jax.ShapeDtypeStruct((bn, C_WIDTH, s), BF16),
            jax.ShapeDtypeStruct((bn, C_WIDTH, s), BF16),
            jax.ShapeDtypeStruct((bn, s, LANES), F32),
            jax.ShapeDtypeStruct((bn, SUBLANES, s), F32),
        ],
        scratch_shapes=[pltpu.VMEM((SUBLANES + tm, PROJ_CHUNK), F32)] * (2 * C_QK_WIDTH // PROJ_CHUNK),
        compiler_params=_params(("arbitrary", "arbitrary")),
        name="proj_c",
    )(x, wqk, wv_t, wo_t, wif, wift, conv_w, conv_b, gb, gbt)


def _split3(a):
    hi = a.astype(BF16)
    r1 = a - hi.astype(F32)
    mid = r1.astype(BF16)
    lo = (r1 - mid.astype(F32)).astype(BF16)
    return hi, mid, lo


def _mlstm_kernel(q_ref, k_ref, vt_ref, ogt_ref, gates_ref, gatest_ref, hg_ref, o_ref,
                  ct_ref, n_ref, m_ref):
    c_idx = pl.program_id(1)
    L = q_ref.shape[1]

    @pl.when(c_idx == 0)
    def _():
        ct_ref[...] = jnp.zeros_like(ct_ref)
        n_ref[...] = jnp.zeros_like(n_ref)
        m_ref[...] = jnp.zeros_like(m_ref)

    src_idx = lax.broadcasted_iota(jnp.int32, (L, L), 0)
    tgt_idx = lax.broadcasted_iota(jnp.int32, (L, L), 1)
    causal = src_idx <= tgt_idx
    tri = (tgt_idx <= src_idx).astype(BF16)
    tri_t = causal.astype(BF16)

    gates = gates_ref[0]
    gates_t = gatest_ref[0]
    b_cols = sum(_dot(tri, part) for part in _split3(jax.nn.log_sigmoid(gates)))
    b_rows = sum(_dot(part, tri_t) for part in _split3(jax.nn.log_sigmoid(gates_t)))

    for h in range(C_HEADS):
        qk_cols = slice(h * C_DQK, (h + 1) * C_DQK)
        v_rows = slice(h * C_DV, (h + 1) * C_DV)
        c_col = gates[:, h:h + 1] - b_cols[:, C_HEADS + h:C_HEADS + h + 1]
        i_row = gates_t[h:h + 1, :]
        b_row = b_rows[C_HEADS + h:C_HEADS + h + 1, :]
        b_end = b_row[:, L - 1:L]
        m_in = m_ref[h][:, 0:1]
        ct_in = ct_ref[h]
        n_in = n_ref[h]

        masked_c = jnp.where(causal, c_col, -jnp.inf)
        m_row = jnp.maximum(m_in, jnp.max(masked_c, axis=0, keepdims=True))
        p_t = jnp.exp(masked_c - m_row)
        inter = jnp.exp(m_in - m_row)

        q_t = q_ref[0, :, qk_cols].astype(F32).T.astype(BF16)
        kh = k_ref[0, :, qk_cols]
        vt_h = vt_ref[0, v_rows, :]
        sqk_t = _dot(kh, q_t) * p_t
        num_t = _dot(vt_h, sqk_t.astype(BF16)) + inter * _dot(ct_in.astype(BF16), q_t)
        n_q = _dot(jnp.broadcast_to(n_in, (SUBLANES, C_DQK)).astype(BF16), q_t)[0:1]
        den = jnp.sum(sqk_t, axis=0, keepdims=True) + inter * n_q
        hid_t = num_t * (1.0 / jnp.maximum(jnp.abs(den), jnp.exp(-(b_row + m_row))))
        mu = jnp.mean(hid_t, axis=0, keepdims=True)
        zc = hid_t - mu
        var = jnp.mean(zc * zc, axis=0, keepdims=True)
        out_t = ogt_ref[0, v_rows, :].astype(F32) * (zc * lax.rsqrt(var + LN_EPS) * hg_ref[v_rows, :])
        o_ref[0, :, v_rows] = out_t.T.astype(BF16)

        a_row = b_end - b_row + i_row
        a_max = jnp.max(a_row, axis=1, keepdims=True)
        w_end = jnp.exp(a_row - a_max)
        m_new = jnp.maximum(b_end + m_in, a_max)
        decay = jnp.exp(b_end + m_in - m_new)
        inject = jnp.exp(a_max - m_new)
        vw = (vt_h.astype(F32) * w_end).astype(BF16)
        ct_ref[h] = decay * ct_in + inject * _dot(vw, kh)
        k_sum = _dot(jnp.broadcast_to(w_end, (SUBLANES, L)).astype(BF16), kh)[0:1]
        n_ref[h] = decay * n_in + inject * k_sum
        m_ref[h] = jnp.broadcast_to(m_new, (1, LANES))


def _mlstm(q, k, v_t, og_t, gates, gates_t, head_g):
    bn, s, _ = q.shape
    L = MLSTM_L
    tok = lambda b, c: (b, c, 0)
    feat = lambda b, c: (b, 0, c)
    hg_cols = jnp.broadcast_to(head_g.reshape(C_WIDTH, 1), (C_WIDTH, L))
    return pl.pallas_call(
        _mlstm_kernel,
        grid=(bn, s // L),
        in_specs=[
            pl.BlockSpec((1, L, C_QK_WIDTH), tok),
            pl.BlockSpec((1, L, C_QK_WIDTH), tok),
            pl.BlockSpec((1, C_WIDTH, L), feat),
            pl.BlockSpec((1, C_WIDTH, L), feat),
            pl.BlockSpec((1, L, LANES), tok),
            pl.BlockSpec((1, SUBLANES, L), feat),
            _resident((C_WIDTH, L)),
        ],
        out_specs=pl.BlockSpec((1, L, C_WIDTH), tok),
        out_shape=jax.ShapeDtypeStruct((bn, s, C_WIDTH), BF16),
        scratch_shapes=[
            pltpu.VMEM((C_HEADS, C_DV, C_DQK), F32),
            pltpu.VMEM((C_HEADS, 1, C_DQK), F32),
            pltpu.VMEM((C_HEADS, 1, LANES), F32),
        ],
        compiler_params=_params(("arbitrary", "arbitrary")),
        name="mlstm",
    )(q, k, v_t, og_t, gates, gates_t, hg_cols)


def _rope_tables(s):
    half = A_DH // 2
    inv = ROPE_THETA ** (-jnp.arange(half, dtype=F32) / half)
    ang = jnp.arange(s).astype(F32)[:, None] * inv[None, :]
    cos, sin = jnp.cos(ang), jnp.sin(ang)
    cos = jnp.tile(cos, (1, LANES // half))
    sin = jnp.tile(jnp.concatenate([-sin, sin], axis=1), (1, LANES // A_DH))
    return cos, sin, cos.T, sin.T


def _mixer_ab(x, w_in, sgu_ln_g, sgu_ln_b, sgu_w, sgu_b):
    bn, s, d = x.shape
    cos, sin, cos_t, sin_t = _rope_tables(s)
    w_kuv = jnp.concatenate([w_in[:, A_WIDTH:2 * A_WIDTH], w_in[:, 3 * A_WIDTH:]], axis=1).astype(BF16)
    w_qv_t = jnp.concatenate([w_in[:, :A_WIDTH], w_in[:, 2 * A_WIDTH:3 * A_WIDTH]], axis=1).T.astype(BF16)
    q_t, k, v_t, u, vn = _proj_ab(x, w_kuv, w_qv_t, cos, sin, cos_t, sin_t,
                                  sgu_ln_g.reshape(1, B_WIDTH), sgu_ln_b.reshape(1, B_WIDTH))
    a = _moba(q_t, k, v_t, _moba_gate(q_t, k))
    bs_full = jnp.repeat(sgu_b.T, B_DG, axis=1)
    bg = _sgu(u, vn, sgu_w, bs_full)
    return [a.reshape(bn * s, A_WIDTH), bg.reshape(bn * s, B_WIDTH)]


def _mixer_c(x, w_in, conv_w, conv_b, b_i, b_f, head_g):
    bn, s, d = x.shape
    qk_w = 2 * C_QK_WIDTH
    wqk = w_in[:, :qk_w].astype(BF16)
    wv_t = w_in[:, qk_w:qk_w + C_WIDTH].T.astype(BF16)
    wo_t = w_in[:, qk_w + C_WIDTH:qk_w + 2 * C_WIDTH].T.astype(BF16)
    w_if = w_in[:, qk_w + 2 * C_WIDTH:]
    wif = jnp.pad(w_if, ((0, 0), (0, LANES - 2 * C_HEADS))).astype(BF16)
    wift = w_if.T.astype(BF16)
    gate_bias = jnp.concatenate([b_i, b_f]).astype(F32)
    gb = jnp.pad(gate_bias, (0, LANES - 2 * C_HEADS)).reshape(1, LANES)
    gbt = gate_bias.reshape(2 * C_HEADS, 1)
    q, k, v_t, og_t, gates, gates_t = _proj_c(x, wqk, wv_t, wo_t, wif, wift, conv_w,
                                              conv_b.reshape(1, qk_w), gb, gbt)
    hg = _mlstm(q, k, v_t, og_t, gates, gates_t, head_g)
    return [hg.reshape(bn * s, C_WIDTH)]


def kernel(x, ln_g, ln_b, ffn_w_gu, ffn_w_down, ab_w_in, sgu_ln_g, sgu_ln_b, sgu_w, sgu_b, ab_w_out,
           c_w_in, c_conv_w, c_conv_b, c_b_i, c_b_f, c_head_g, c_w_out):
    bn, s, d = x.shape
    n = bn * s

    w_gu = ffn_w_gu.astype(BF16)
    w_down = ffn_w_down.astype(BF16)
    norm = lambda l, i: (ln_g[l, i].reshape(1, d), ln_b[l, i].reshape(1, d))
    x = x.reshape(n, d)
    for l in range(DEPTH):
        x = _ffn_ln(x, w_gu[l, 0], w_down[l, 0], *norm(l, 0))
        j = l // 2
        if l % 2 == 0:
            parts = _mixer_ab(x.reshape(bn, s, d), ab_w_in[j], sgu_ln_g[j], sgu_ln_b[j], sgu_w[j], sgu_b[j])
            w_out = ab_w_out[j]
        else:
            parts = _mixer_c(x.reshape(bn, s, d), c_w_in[j], c_conv_w[j], c_conv_b[j], c_b_i[j], c_b_f[j],
                             c_head_g[j])
            w_out = c_w_out[j]
        x = _out_ffn(x, parts, w_out.astype(BF16), *norm(l, 1), w_gu[l, 1], w_down[l, 1], *norm(l, 2))
    return x.reshape(bn, s, d)
```

```python
import functools

import jax
import jax.numpy as jnp
from jax import lax
from jax.experimental import pallas as pl
from jax.experimental.pallas import tpu as pltpu

D_MODEL = 1024
DEPTH = 2
D_FF = 2816
ALPHA = (2 * DEPTH) ** 0.25
LN_EPS = 1e-5
NEG = -1e30

A_HEADS = 8
A_DH = 64
A_WIDTH = A_HEADS * A_DH
MOBA_BLOCK = 256
MOBA_TOPK = 3
ROPE_THETA = 10000.0

B_GROUPS = 8
B_DG = 64
B_WIDTH = B_GROUPS * B_DG
SGU_CHUNK = 128

C_HEADS = 4
C_DQK = 128
C_DV = 256
C_QK_WIDTH = C_HEADS * C_DQK
C_WIDTH = C_HEADS * C_DV
CONV_W = 4

LANES = 128
SUBLANES = 8
VMEM_LIMIT = 56 * 1024 * 1024

TOKEN_TILE = 512
PROJ_TILE = 1024
FF_CHUNK = 256
PROJ_CHUNK = 256
MLSTM_L = 256
MOBA_RING = 3
MOBA_PAIRS = 4
MOBA_SUM_ROWS = 16
MOBA_GATE_WIDTH = 1024
LOG2E = 1.4426950408889634

BF16 = jnp.bfloat16
F32 = jnp.float32


def _params(sem):
    return pltpu.CompilerParams(dimension_semantics=sem, vmem_limit_bytes=VMEM_LIMIT)


def _dot(a, b):
    return jnp.dot(a, b, preferred_element_type=F32)


def _dot_nt(a, b):
    return lax.dot_general(a, b, (((1,), (1,)), ((), ())), preferred_element_type=F32)


def _layer_norm(z, g, b):
    mu = jnp.mean(z, axis=-1, keepdims=True)
    zc = z - mu
    var = jnp.mean(zc * zc, axis=-1, keepdims=True)
    y = zc * lax.rsqrt(var + LN_EPS) * g
    return y if b is None else y + b


def _resident(shape):
    return pl.BlockSpec(shape, lambda *_: (0,) * len(shape), pipeline_mode=pl.Buffered(1))


def _ffn_ln_value(x, wgu_ref, wd_ref, g_ref, b_ref, acc_ref):
    xb = x.astype(BF16)
    for c in range(D_FF // FF_CHUNK):
        lo = c * FF_CHUNK
        gate = _dot(xb, wgu_ref[:, lo:lo + FF_CHUNK])
        up = _dot(xb, wgu_ref[:, D_FF + lo:D_FF + lo + FF_CHUNK])
        h = (jax.nn.silu(gate) * up).astype(BF16)
        part = _dot(h, wd_ref[lo:lo + FF_CHUNK, :])
        if c == 0:
            acc_ref[...] = part
        else:
            acc_ref[...] += part
    z = ALPHA * x + 0.5 * acc_ref[...]
    return _layer_norm(z, g_ref[...], b_ref[...])


def _ffn_ln_kernel(x_ref, wgu_ref, wd_ref, g_ref, b_ref, o_ref, acc_ref):
    o_ref[...] = _ffn_ln_value(x_ref[...], wgu_ref, wd_ref, g_ref, b_ref, acc_ref)


def _ffn_ln(x2, w_gu, w_down, g, b):
    n = x2.shape[0]
    tm = TOKEN_TILE
    return pl.pallas_call(
        _ffn_ln_kernel,
        grid=(n // tm,),
        in_specs=[pl.BlockSpec((tm, D_MODEL), lambda i: (i, 0))]
        + [_resident(a.shape) for a in (w_gu, w_down, g, b)],
        out_specs=pl.BlockSpec((tm, D_MODEL), lambda i: (i, 0)),
        out_shape=jax.ShapeDtypeStruct((n, D_MODEL), F32),
        scratch_shapes=[pltpu.VMEM((tm, D_MODEL), F32)],
        compiler_params=_params(("arbitrary",)),
        name="ffn_ln",
    )(x2, w_gu, w_down, g, b)


def _swap_halves_lanes(t, first_half):
    fwd = pltpu.roll(t, A_DH // 2, axis=1)
    bwd = pltpu.roll(t, LANES - A_DH // 2, axis=1)
    return jnp.where(first_half, bwd, fwd)


def _swap_halves_rows(t):
    half = A_DH // 2
    return jnp.concatenate([t[half:2 * half], t[:half], t[3 * half:], t[2 * half:3 * half]], axis=0)


def _proj_ab_kernel(x_ref, w_ref, wqvt_ref, cos_ref, sin_ref, cost_ref, sint_ref, lng_ref, lnb_ref,
                    qt_ref, k_ref, vt_ref, u_ref, vn_ref):
    xb = x_ref[0].astype(BF16)
    cos = cos_ref[...]
    sin = sin_ref[...]
    cos_t = cost_ref[...]
    sin_t = sint_ref[...]
    lane = lax.broadcasted_iota(jnp.int32, cos.shape, 1)
    first_half = (lane % A_DH) < (A_DH // 2)
    q_t = _dot_nt(wqvt_ref[:A_WIDTH, :], xb)
    for grp in range(A_WIDTH // LANES):
        lo = grp * LANES
        qg = q_t[lo:lo + LANES]
        qg = qg * cos_t + _swap_halves_rows(qg) * sin_t
        qt_ref[0, lo:lo + LANES, :] = (qg * (A_DH ** -0.5 * LOG2E)).astype(BF16)
        k = _dot(xb, w_ref[:, lo:lo + LANES])
        k = k * cos + _swap_halves_lanes(k, first_half) * sin
        k_ref[0, :, lo:lo + LANES] = k.astype(BF16)
    vt_ref[0] = _dot_nt(wqvt_ref[A_WIDTH:, :], xb).astype(BF16)
    ub = _dot(xb, w_ref[:, A_WIDTH:A_WIDTH + B_WIDTH])
    u_ref[0] = jax.nn.gelu(ub).astype(BF16)
    vb = _dot(xb, w_ref[:, A_WIDTH + B_WIDTH:])
    vn_ref[0] = _layer_norm(jax.nn.gelu(vb), lng_ref[...], lnb_ref[...]).astype(BF16)


def _proj_ab(x, w_kuv, w_qv_t, cos, sin, cos_t, sin_t, lng, lnb):
    bn, s, _ = x.shape
    tm = min(s, PROJ_TILE)
    const = lambda b, t: (0, 0)
    tok = lambda b, t: (b, t, 0)
    tok_t = lambda b, t: (b, 0, t)
    out = jax.ShapeDtypeStruct((bn, s, A_WIDTH), BF16)
    out_t = jax.ShapeDtypeStruct((bn, A_WIDTH, s), BF16)
    return pl.pallas_call(
        _proj_ab_kernel,
        grid=(bn, s // tm),
        in_specs=[
            pl.BlockSpec((1, tm, D_MODEL), tok),
            _resident(w_kuv.shape),
            _resident(w_qv_t.shape),
            pl.BlockSpec((tm, LANES), lambda b, t: (t, 0)),
            pl.BlockSpec((tm, LANES), lambda b, t: (t, 0)),
            pl.BlockSpec((LANES, tm), lambda b, t: (0, t)),
            pl.BlockSpec((LANES, tm), lambda b, t: (0, t)),
            pl.BlockSpec((1, B_WIDTH), const),
            pl.BlockSpec((1, B_WIDTH), const),
        ],
        out_specs=[pl.BlockSpec((1, A_WIDTH, tm), tok_t), pl.BlockSpec((1, tm, A_WIDTH), tok),
                   pl.BlockSpec((1, A_WIDTH, tm), tok_t), pl.BlockSpec((1, tm, B_WIDTH), tok),
                   pl.BlockSpec((1, tm, B_WIDTH), tok)],
        out_shape=[out_t, out, out_t, out, out],
        compiler_params=_params(("arbitrary", "arbitrary")),
        name="proj_ab",
    )(x, w_kuv, w_qv_t, cos, sin, cos_t, sin_t, lng, lnb)


def _moba_gate_kernel(qt_ref, k_ref, avg_ref, bias_ref):
    blk = MOBA_BLOCK
    gate_rows = avg_ref.shape[0]
    s_len = qt_ref.shape[2]
    width = min(s_len, MOBA_GATE_WIDTH)
    kmean = _dot(avg_ref[...], k_ref[0])
    lane_g = lax.broadcasted_iota(jnp.int32, kmean.shape, 1)
    blk_row = lax.broadcasted_iota(jnp.int32, (gate_rows, width), 0)
    feat_row = lax.broadcasted_iota(jnp.int32, (LANES, width), 0)
    qry_lane = lax.broadcasted_iota(jnp.int32, (1, width), 1)
    for c in range(s_len // width):
        q_pair = qt_ref[0, :, c * width:(c + 1) * width]
        q_blk = (qry_lane + c * width) // blk
        for h in range(2):
            head_rows = (feat_row < A_DH) if h == 0 else (feat_row >= A_DH)
            head_lanes = (lane_g < A_DH) if h == 0 else (lane_g >= A_DH)
            qh = jnp.where(head_rows, q_pair, jnp.zeros_like(q_pair))
            gate = _dot(jnp.where(head_lanes, kmean, 0.0).astype(BF16), qh)
            gate = jnp.where(blk_row < q_blk, gate, NEG)
            chosen = jnp.zeros((gate_rows, width), jnp.bool_)
            for _ in range(MOBA_TOPK):
                best = jnp.max(gate, axis=0, keepdims=True)
                idx = jnp.min(jnp.where(gate == best, blk_row, LANES), axis=0, keepdims=True)
                pick = blk_row == idx
                chosen = chosen | (pick & (idx < q_blk))
                gate = jnp.where(pick, -jnp.inf, gate)
            bias_ref[0, h, :, c * width:(c + 1) * width] = jnp.where(chosen, 0.0, NEG).astype(BF16)


def _moba_gate(q_t, k):
    bn, s, _ = k.shape
    blk = MOBA_BLOCK
    nb = s // blk
    assert nb <= LANES
    gate_rows = -(-nb // 16) * 16
    block_avg = ((jnp.arange(gate_rows)[:, None] == jnp.arange(s)[None, :] // blk) / blk).astype(BF16)
    return pl.pallas_call(
        _moba_gate_kernel,
        grid=(bn, A_WIDTH // LANES),
        in_specs=[
            pl.BlockSpec((1, LANES, s), lambda b, p: (b, p, 0)),
            pl.BlockSpec((1, s, LANES), lambda b, p: (b, 0, p)),
            _resident((gate_rows, s)),
        ],
        out_specs=pl.BlockSpec((1, 2, gate_rows, s), lambda b, p: (b, p, 0, 0)),
        out_shape=jax.ShapeDtypeStruct((bn, A_HEADS, gate_rows, s), BF16),
        compiler_params=_params(("arbitrary", "arbitrary")),
        name="moba_gate",
    )(q_t, k, block_avg)


def _moba_kernel(qt_ref, k_ref, vt_ref, bias_ref, e_ref, o_ref, *ring):
    i = pl.program_id(2)
    blk = MOBA_BLOCK
    nb = k_ref.shape[1] // blk
    pairs = qt_ref.shape[1] // LANES
    depth = len(ring) // pairs
    ahead = depth - 2
    gate_rows = bias_ref.shape[2]
    feat_row = lax.broadcasted_iota(jnp.int32, (LANES, blk), 0)
    key_idx = lax.broadcasted_iota(jnp.int32, (blk, blk), 0)
    qry_idx = lax.broadcasted_iota(jnp.int32, (blk, blk), 1)
    causal = key_idx <= qry_idx
    bias_pad = jnp.zeros((LANES - gate_rows, blk), BF16)
    ones_rows = jnp.ones((MOBA_SUM_ROWS, blk), BF16)
    own = pl.multiple_of(i * blk, blk)

    def key_start(j):
        return pl.multiple_of(jnp.minimum(j, nb - 1) * blk, blk)

    def values(head, start):
        return jnp.concatenate([vt_ref[0, head * A_DH:(head + 1) * A_DH, pl.ds(start, blk)], ones_rows], axis=0)

    def absorb(pair, s_both, start, carry, mask=None):
        out = []
        for h in range(2):
            s = s_both[:, h * blk:(h + 1) * blk]
            if mask is not None:
                s = jnp.where(mask, s, NEG)
            m_new = jnp.max(s, axis=0, keepdims=True)
            pv = lambda p: _dot(values(2 * pair + h, start), p.astype(BF16))
            if carry is None:
                out += [m_new, pv(jnp.exp2(s - m_new))]
            else:
                m_old, a_old = carry[2 * h], carry[2 * h + 1]
                m_new = jnp.maximum(m_old, m_new)
                out += [m_new, jnp.exp2(m_old - m_new) * a_old + pv(jnp.exp2(s - m_new))]
        return out

    rhs, carry = [], []
    for pair in range(pairs):
        rows = slice(pair * LANES, (pair + 1) * LANES)
        q_pair = qt_ref[0, rows, :]
        qh = [jnp.where(feat_row < A_DH, q_pair, jnp.zeros_like(q_pair)),
              jnp.where(feat_row >= A_DH, q_pair, jnp.zeros_like(q_pair))]
        rhs.append(jnp.concatenate([jnp.concatenate([qh[h], bias_ref[0, 2 * pair + h], bias_pad], axis=0)
                                    for h in range(2)], axis=1))
        s_own = _dot(k_ref[0, pl.ds(own, blk), rows], jnp.concatenate(qh, axis=1))
        carry += absorb(pair, s_own, own, None, causal)

    def scores(pair, j):
        start = key_start(j)
        lhs = jnp.concatenate([k_ref[0, pl.ds(start, blk), pair * LANES:(pair + 1) * LANES],
                               e_ref[pl.ds(start, blk), :]], axis=1)
        return _dot(lhs, rhs[pair])

    for pair in range(pairs):
        for r in range(ahead):
            ring[pair * depth + r][...] = scores(pair, r)

    def body(it, carry):
        carry = list(carry)
        for r in range(depth):
            j = depth * it + r
            for pair in range(pairs):
                ring[pair * depth + (r + ahead) % depth][...] = scores(pair, j + ahead)
                carry[4 * pair:4 * pair + 4] = absorb(pair, ring[pair * depth + r][...], key_start(j),
                                                      carry[4 * pair:4 * pair + 4])
        return tuple(carry)

    carry = lax.fori_loop(0, (i + depth - 1) // depth, body, tuple(carry))
    out_t = jnp.concatenate([acc[:A_DH] / acc[A_DH:A_DH + 1] for acc in carry[1::2]], axis=0)
    o_ref[0] = out_t.T.astype(BF16)


def _moba(q_t, k, v_t, bias):
    bn, s, _ = k.shape
    blk = MOBA_BLOCK
    nb = s // blk
    gate_rows = bias.shape[2]
    width = MOBA_PAIRS * LANES
    block_onehot = (jnp.arange(s)[:, None] // blk == jnp.arange(LANES)[None, :]).astype(BF16)
    return pl.pallas_call(
        _moba_kernel,
        grid=(bn, A_WIDTH // width, nb),
        in_specs=[
            pl.BlockSpec((1, width, blk), lambda b, p, i: (b, p, i)),
            pl.BlockSpec((1, s, width), lambda b, p, i: (b, 0, p)),
            pl.BlockSpec((1, width, s), lambda b, p, i: (b, p, 0)),
            pl.BlockSpec((1, 2 * MOBA_PAIRS, gate_rows, blk), lambda b, p, i: (b, p, 0, i)),
            _resident((s, LANES)),
        ],
        out_specs=pl.BlockSpec((1, blk, width), lambda b, p, i: (b, i, p)),
        out_shape=jax.ShapeDtypeStruct((bn, s, A_WIDTH), BF16),
        scratch_shapes=[pltpu.VMEM((blk, 2 * blk), F32)] * (MOBA_PAIRS * MOBA_RING),
        compiler_params=_params(("arbitrary", "arbitrary", "arbitrary")),
        name="moba",
    )(q_t, k, v_t, bias, block_onehot)


def _sgu_kernel(u_ref, vn_ref, w_ref, bs_ref, o_ref):
    tm = u_ref.shape[1]
    t_row = lax.broadcasted_iota(jnp.int32, (SGU_CHUNK, SGU_CHUNK), 0)
    t_col = lax.broadcasted_iota(jnp.int32, (SGU_CHUNK, SGU_CHUNK), 1)
    causal = t_col <= t_row
    lane = lax.broadcasted_iota(jnp.int32, (SGU_CHUNK, LANES), 1)
    lo_mask = lane < B_DG
    for pair in range(B_WIDTH // LANES):
        w_lo = jnp.where(causal, w_ref[2 * pair], 0.0).astype(BF16)
        w_hi = jnp.where(causal, w_ref[2 * pair + 1], 0.0).astype(BF16)
        cols = slice(pair * LANES, (pair + 1) * LANES)
        bias = bs_ref[:, cols]
        for c in range(tm // SGU_CHUNK):
            rows = slice(c * SGU_CHUNK, (c + 1) * SGU_CHUNK)
            vp = vn_ref[0, rows, cols]
            zero = jnp.zeros_like(vp)
            mixed = _dot(w_lo, jnp.where(lo_mask, vp, zero)) + _dot(w_hi, jnp.where(lo_mask, zero, vp))
            o_ref[0, rows, cols] = (u_ref[0, rows, cols].astype(F32) * (mixed + bias)).astype(BF16)


def _sgu(u, vn, w_s, bs_full):
    bn, s, _ = u.shape
    tm = TOKEN_TILE
    tok = lambda b, t: (b, t, 0)
    return pl.pallas_call(
        _sgu_kernel,
        grid=(bn, s // tm),
        in_specs=[
            pl.BlockSpec((1, tm, B_WIDTH), tok),
            pl.BlockSpec((1, tm, B_WIDTH), tok),
            pl.BlockSpec(w_s.shape, lambda b, t: (0, 0, 0)),
            pl.BlockSpec(bs_full.shape, lambda b, t: (0, 0)),
        ],
        out_specs=pl.BlockSpec((1, tm, B_WIDTH), tok),
        out_shape=jax.ShapeDtypeStruct((bn, s, B_WIDTH), BF16),
        compiler_params=_params(("arbitrary", "arbitrary")),
        name="sgu",
    )(u, vn, w_s, bs_full)


def _out_ffn_kernel(n_parts, *refs):
    x_ref = refs[0]
    part_refs = refs[1:1 + n_parts]
    w_ref, g1_ref, b1_ref, wgu_ref, wd_ref, g2_ref, b2_ref, o_ref, acc_ref = refs[1 + n_parts:]
    y = None
    lo = 0
    for p_ref in part_refs:
        width = p_ref.shape[-1]
        term = _dot(p_ref[...], w_ref[lo:lo + width, :])
        y = term if y is None else y + term
        lo += width
    mid = _layer_norm(ALPHA * x_ref[...] + y, g1_ref[...], b1_ref[...])
    o_ref[...] = _ffn_ln_value(mid, wgu_ref, wd_ref, g2_ref, b2_ref, acc_ref)


def _out_ffn(x2, parts, w_out, g1, b1, w_gu, w_down, g2, b2):
    n = x2.shape[0]
    tm = TOKEN_TILE
    tok = lambda i: (i, 0)
    consts = (w_out, g1, b1, w_gu, w_down, g2, b2)
    return pl.pallas_call(
        functools.partial(_out_ffn_kernel, len(parts)),
        grid=(n // tm,),
        in_specs=[pl.BlockSpec((tm, D_MODEL), tok)]
        + [pl.BlockSpec((tm, p.shape[-1]), tok) for p in parts]
        + [_resident(a.shape) for a in consts],
        out_specs=pl.BlockSpec((tm, D_MODEL), tok),
        out_shape=jax.ShapeDtypeStruct((n, D_MODEL), F32),
        scratch_shapes=[pltpu.VMEM((tm, D_MODEL), F32)],
        compiler_params=_params(("arbitrary",)),
        name="out_ffn",
    )(x2, *parts, *consts)


def _proj_c_kernel(x_ref, wqk_ref, wvt_ref, wot_ref, wif_ref, wift_ref, cw_ref, cb_ref, gb_ref, gbt_ref,
                   q_ref, k_ref, vt_ref, ogt_ref, gates_ref, gatest_ref, *pre_refs):
    t = pl.program_id(1)
    tm = x_ref.shape[1]
    xb = x_ref[0].astype(BF16)
    width = pre_refs[0].shape[1]

    @pl.when(t == 0)
    def _():
        for pre_ref in pre_refs:
            pre_ref[:SUBLANES, :] = jnp.zeros((SUBLANES, width), F32)

    gates_ref[0] = _dot(xb, wif_ref[...]) + gb_ref[...]
    gatest_ref[0] = _dot_nt(wift_ref[...], xb) + gbt_ref[...]
    for c, pre_ref in enumerate(pre_refs):
        pre_ref[SUBLANES:, :] = _dot(xb, wqk_ref[:, c * width:(c + 1) * width])
    for c, pre_ref in enumerate(pre_refs):
        cols = slice(c * width, (c + 1) * width)
        vt_ref[0, cols, :] = _dot_nt(wvt_ref[cols, :], xb).astype(BF16)
        conv = cb_ref[:, cols] + cw_ref[CONV_W - 1:CONV_W, cols] * pre_ref[SUBLANES:, :]
        for d in range(1, CONV_W):
            conv = conv + cw_ref[CONV_W - 1 - d:CONV_W - d, cols] * pre_ref[SUBLANES - d:SUBLANES - d + tm, :]
        pre_ref[:SUBLANES, :] = pre_ref[tm:, :]
        act = jax.nn.silu(conv)
        if c * width < C_QK_WIDTH:
            q_ref[0, :, cols] = act.astype(BF16)
        else:
            k_cols = slice(c * width - C_QK_WIDTH, (c + 1) * width - C_QK_WIDTH)
            k_ref[0, :, k_cols] = (act * (C_DQK ** -0.5)).astype(BF16)
        ogt_ref[0, cols, :] = jax.nn.sigmoid(_dot_nt(wot_ref[cols, :], xb)).astype(BF16)


def _proj_c(x, wqk, wv_t, wo_t, wif, wift, conv_w, conv_b, gb, gbt):
    bn, s, _ = x.shape
    tm = min(s, PROJ_TILE)
    tok = lambda b, t: (b, t, 0)
    return pl.pallas_call(
        _proj_c_kernel,
        grid=(bn, s // tm),
        in_specs=[pl.BlockSpec((1, tm, D_MODEL), tok)]
        + [_resident(a.shape) for a in (wqk, wv_t, wo_t, wif, wift, conv_w, conv_b, gb, gbt)],
        out_specs=[
            pl.BlockSpec((1, tm, C_QK_WIDTH), tok),
            pl.BlockSpec((1, tm, C_QK_WIDTH), tok),
            pl.BlockSpec((1, C_WIDTH, tm), lambda b, t: (b, 0, t)),
            pl.BlockSpec((1, C_WIDTH, tm), lambda b, t: (b, 0, t)),
            pl.BlockSpec((1, tm, LANES), tok),
            pl.BlockSpec((1, SUBLANES, tm), lambda b, t: (b, 0, t)),
        ],
        out_shape=[
            jax.ShapeDtypeStruct((bn, s, C_QK_WIDTH), BF16),
            jax.ShapeDtypeStruct((bn, s, C_QK_WIDTH), BF16),
            jax.ShapeDtypeStruct((bn, C_WIDTH, s), BF16),
            jax.ShapeDtypeStruct((bn, C_WIDTH, s), BF16),
            jax.ShapeDtypeStruct((bn, s, LANES), F32),
            jax.ShapeDtypeStruct((bn, SUBLANES, s), F32),
        ],
        scratch_shapes=[pltpu.VMEM((SUBLANES + tm, PROJ_CHUNK), F32)] * (2 * C_QK_WIDTH // PROJ_CHUNK),
        compiler_params=_params(("arbitrary", "arbitrary")),
        name="proj_c",
    )(x, wqk, wv_t, wo_t, wif, wift, conv_w, conv_b, gb, gbt)


def _split3(a):
    hi = a.astype(BF16)
    r1 = a - hi.astype(F32)
    mid = r1.astype(BF16)
    lo = (r1 - mid.astype(F32)).astype(BF16)
    return hi, mid, lo


def _mlstm_kernel(q_ref, k_ref, vt_ref, ogt_ref, gates_ref, gatest_ref, hg_ref, o_ref,
                  ct_ref, n_ref, m_ref):
    c_idx = pl.program_id(1)
    L = q_ref.shape[1]

    @pl.when(c_idx == 0)
    def _():
        ct_ref[...] = jnp.zeros_like(ct_ref)
        n_ref[...] = jnp.zeros_like(n_ref)
        m_ref[...] = jnp.zeros_like(m_ref)

    src_idx = lax.broadcasted_iota(jnp.int32, (L, L), 0)
    tgt_idx = lax.broadcasted_iota(jnp.int32, (L, L), 1)
    causal = src_idx <= tgt_idx
    tri = (tgt_idx <= src_idx).astype(BF16)
    tri_t = causal.astype(BF16)

    gates = gates_ref[0]
    gates_t = gatest_ref[0]
    b_cols = sum(_dot(tri, part) for part in _split3(jax.nn.log_sigmoid(gates)))
    b_rows = sum(_dot(part, tri_t) for part in _split3(jax.nn.log_sigmoid(gates_t)))

    for h in range(C_HEADS):
        qk_cols = slice(h * C_DQK, (h + 1) * C_DQK)
        v_rows = slice(h * C_DV, (h + 1) * C_DV)
        c_col = gates[:, h:h + 1] - b_cols[:, C_HEADS + h:C_HEADS + h + 1]
        i_row = gates_t[h:h + 1, :]
        b_row = b_rows[C_HEADS + h:C_HEADS + h + 1, :]
        b_end = b_row[:, L - 1:L]
        m_in = m_ref[h][:, 0:1]
        ct_in = ct_ref[h]
        n_in = n_ref[h]

        masked_c = jnp.where(causal, c_col, -jnp.inf)
        m_row = jnp.maximum(m_in, jnp.max(masked_c, axis=0, keepdims=True))
        p_t = jnp.exp(masked_c - m_row)
        inter = jnp.exp(m_in - m_row)

        q_t = q_ref[0, :, qk_cols].astype(F32).T.astype(BF16)
        kh = k_ref[0, :, qk_cols]
        vt_h = vt_ref[0, v_rows, :]
        sqk_t = _dot(kh, q_t) * p_t
        num_t = _dot(vt_h, sqk_t.astype(BF16)) + inter * _dot(ct_in.astype(BF16), q_t)
        n_q = _dot(jnp.broadcast_to(n_in, (SUBLANES, C_DQK)).astype(BF16), q_t)[0:1]
        den = jnp.sum(sqk_t, axis=0, keepdims=True) + inter * n_q
        hid_t = num_t * (1.0 / jnp.maximum(jnp.abs(den), jnp.exp(-(b_row + m_row))))
        mu = jnp.mean(hid_t, axis=0, keepdims=True)
        zc = hid_t - mu
        var = jnp.mean(zc * zc, axis=0, keepdims=True)
        out_t = ogt_ref[0, v_rows, :].astype(F32) * (zc * lax.rsqrt(var + LN_EPS) * hg_ref[v_rows, :])
        o_ref[0, :, v_rows] = out_t.T.astype(BF16)

        a_row = b_end - b_row + i_row
        a_max = jnp.max(a_row, axis=1, keepdims=True)
        w_end = jnp.exp(a_row - a_max)
        m_new = jnp.maximum(b_end + m_in, a_max)
        decay = jnp.exp(b_end + m_in - m_new)
        inject = jnp.exp(a_max - m_new)
        vw = (vt_h.astype(F32) * w_end).astype(BF16)
        ct_ref[h] = decay * ct_in + inject * _dot(vw, kh)
        k_sum = _dot(jnp.broadcast_to(w_end, (SUBLANES, L)).astype(BF16), kh)[0:1]
        n_ref[h] = decay * n_in + inject * k_sum
        m_ref[h] = jnp.broadcast_to(m_new, (1, LANES))


def _mlstm(q, k, v_t, og_t, gates, gates_t, head_g):
    bn, s, _ = q.shape
    L = MLSTM_L
    tok = lambda b, c: (b, c, 0)
    feat = lambda b, c: (b, 0, c)
    hg_cols = jnp.broadcast_to(head_g.reshape(C_WIDTH, 1), (C_WIDTH, L))
    return pl.pallas_call(
        _mlstm_kernel,
        grid=(bn, s // L),
        in_specs=[
            pl.BlockSpec((1, L, C_QK_WIDTH), tok),
            pl.BlockSpec((1, L, C_QK_WIDTH), tok),
            pl.BlockSpec((1, C_WIDTH, L), feat),
            pl.BlockSpec((1, C_WIDTH, L), feat),
            pl.BlockSpec((1, L, LANES), tok),
            pl.BlockSpec((1, SUBLANES, L), feat),
            _resident((C_WIDTH, L)),
        ],
        out_specs=pl.BlockSpec((1, L, C_WIDTH), tok),
        out_shape=jax.ShapeDtypeStruct((bn, s, C_WIDTH), BF16),
        scratch_shapes=[
            pltpu.VMEM((C_HEADS, C_DV, C_DQK), F32),
            pltpu.VMEM((C_HEADS, 1, C_DQK), F32),
            pltpu.VMEM((C_HEADS, 1, LANES), F32),
        ],
        compiler_params=_params(("arbitrary", "arbitrary")),
        name="mlstm",
    )(q, k, v_t, og_t, gates, gates_t, hg_cols)


def _rope_tables(s):
    half = A_DH // 2
    inv = ROPE_THETA ** (-jnp.arange(half, dtype=F32) / half)
    ang = jnp.arange(s).astype(F32)[:, None] * inv[None, :]
    cos, sin = jnp.cos(ang), jnp.sin(ang)
    cos = jnp.tile(cos, (1, LANES // half))
    sin = jnp.tile(jnp.concatenate([-sin, sin], axis=1), (1, LANES // A_DH))
    return cos, sin, cos.T, sin.T


def _mixer_ab(x, w_in, sgu_ln_g, sgu_ln_b, sgu_w, sgu_b):
    bn, s, d = x.shape
    cos, sin, cos_t, sin_t = _rope_tables(s)
    w_kuv = jnp.concatenate([w_in[:, A_WIDTH:2 * A_WIDTH], w_in[:, 3 * A_WIDTH:]], axis=1).astype(BF16)
    w_qv_t = jnp.concatenate([w_in[:, :A_WIDTH], w_in[:, 2 * A_WIDTH:3 * A_WIDTH]], axis=1).T.astype(BF16)
    q_t, k, v_t, u, vn = _proj_ab(x, w_kuv, w_qv_t, cos, sin, cos_t, sin_t,
                                  sgu_ln_g.reshape(1, B_WIDTH), sgu_ln_b.reshape(1, B_WIDTH))
    a = _moba(q_t, k, v_t, _moba_gate(q_t, k))
    bs_full = jnp.repeat(sgu_b.T, B_DG, axis=1)
    bg = _sgu(u, vn, sgu_w, bs_full)
    return [a.reshape(bn * s, A_WIDTH), bg.reshape(bn * s, B_WIDTH)]


def _mixer_c(x, w_in, conv_w, conv_b, b_i, b_f, head_g):
    bn, s, d = x.shape
    qk_w = 2 * C_QK_WIDTH
    wqk = w_in[:, :qk_w].astype(BF16)
    wv_t = w_in[:, qk_w:qk_w + C_WIDTH].T.astype(BF16)
    wo_t = w_in[:, qk_w + C_WIDTH:qk_w + 2 * C_WIDTH].T.astype(BF16)
    w_if = w_in[:, qk_w + 2 * C_WIDTH:]
    wif = jnp.pad(w_if, ((0, 0), (0, LANES - 2 * C_HEADS))).astype(BF16)
    wift = w_if.T.astype(BF16)
    gate_bias = jnp.concatenate([b_i, b_f]).astype(F32)
    gb = jnp.pad(gate_bias, (0, LANES - 2 * C_HEADS)).reshape(1, LANES)
    gbt = gate_bias.reshape(2 * C_HEADS, 1)
    q, k, v_t, og_t, gates, gates_t = _proj_c(x, wqk, wv_t, wo_t, wif, wift, conv_w,
                                              conv_b.reshape(1, qk_w), gb, gbt)
    hg = _mlstm(q, k, v_t, og_t, gates, gates_t, head_g)
    return [hg.reshape(bn * s, C_WIDTH)]


def kernel(x, ln_g, ln_b, ffn_w_gu, ffn_w_down, ab_w_in, sgu_ln_g, sgu_ln_b, sgu_w, sgu_b, ab_w_out,
           c_w_in, c_conv_w, c_conv_b, c_b_i, c_b_f, c_head_g, c_w_out):
    bn, s, d = x.shape
    n = bn * s

    w_gu = ffn_w_gu.astype(BF16)
    w_down = ffn_w_down.astype(BF16)
    norm = lambda l, i: (ln_g[l, i].reshape(1, d), ln_b[l, i].reshape(1, d))
    x = x.reshape(n, d)
    for l in range(DEPTH):
        x = _ffn_ln(x, w_gu[l, 0], w_down[l, 0], *norm(l, 0))
        j = l // 2
        if l % 2 == 0:
            parts = _mixer_ab(x.reshape(bn, s, d), ab_w_in[j], sgu_ln_g[j], sgu_ln_b[j], sgu_w[j], sgu_b[j])
            w_out = ab_w_out[j]
        else:
            parts = _mixer_c(x.reshape(bn, s, d), c_w_in[j], c_conv_w[j], c_conv_b[j], c_b_i[j], c_b_f[j],
                             c_head_g[j])
            w_out = c_w_out[j]
        x = _out_ffn(x, parts, w_out.astype(BF16), *norm(l, 1), w_gu[l, 1], w_down[l, 1], *norm(l, 2))
    return x.reshape(bn, s, d)
```

```python
import functools

import jax
import jax.numpy as jnp
from jax import lax
from jax.experimental import pallas as pl
from jax.experimental.pallas import tpu as pltpu

D_MODEL = 1024
DEPTH = 2
D_FF = 2816
ALPHA = (2 * DEPTH) ** 0.25
LN_EPS = 1e-5
NEG = -1e30

A_HEADS = 8
A_DH = 64
A_WIDTH = A_HEADS * A_DH
MOBA_BLOCK = 256
MOBA_TOPK = 3
ROPE_THETA = 10000.0

B_GROUPS = 8
B_DG = 64
B_WIDTH = B_GROUPS * B_DG
SGU_CHUNK = 128

C_HEADS = 4
C_DQK = 128
C_DV = 256
C_QK_WIDTH = C_HEADS * C_DQK
C_WIDTH = C_HEADS * C_DV
CONV_W = 4

LANES = 128
SUBLANES = 8
VMEM_LIMIT = 56 * 1024 * 1024

TOKEN_TILE = 512
PROJ_TILE = 1024
FF_CHUNK = 256
PROJ_CHUNK = 256
MLSTM_L = 256
MOBA_RING = 3
MOBA_PAIRS = 4
MOBA_SUM_ROWS = 16
MOBA_GATE_WIDTH = 1024
LOG2E = 1.4426950408889634

BF16 = jnp.bfloat16
F32 = jnp.float32


def _params(sem):
    return pltpu.CompilerParams(dimension_semantics=sem, vmem_limit_bytes=VMEM_LIMIT)


def _dot(a, b):
    return jnp.dot(a, b, preferred_element_type=F32)


def _dot_nt(a, b):
    return lax.dot_general(a, b, (((1,), (1,)), ((), ())), preferred_element_type=F32)


def _layer_norm(z, g, b):
    mu = jnp.mean(z, axis=-1, keepdims=True)
    zc = z - mu
    var = jnp.mean(zc * zc, axis=-1, keepdims=True)
    y = zc * lax.rsqrt(var + LN_EPS) * g
    return y if b is None else y + b


def _resident(shape):
    return pl.BlockSpec(shape, lambda *_: (0,) * len(shape), pipeline_mode=pl.Buffered(1))


def _picked(arr, lead):
    rest = arr.shape[len(lead):]
    return pl.BlockSpec((None,) * len(lead) + rest, lambda *_: tuple(lead) + (0,) * len(rest),
                        pipeline_mode=pl.Buffered(1))


def _ffn_ln_value(x, wgu_ref, wd_ref, g_ref, b_ref, acc_ref):
    xb = x.astype(BF16)
    for c in range(D_FF // FF_CHUNK):
        lo = c * FF_CHUNK
        gate = _dot(xb, wgu_ref[:, lo:lo + FF_CHUNK])
        up = _dot(xb, wgu_ref[:, D_FF + lo:D_FF + lo + FF_CHUNK])
        h = (jax.nn.silu(gate) * up).astype(BF16)
        part = _dot(h, wd_ref[lo:lo + FF_CHUNK, :])
        if c == 0:
            acc_ref[...] = part
        else:
            acc_ref[...] += part
    z = ALPHA * x + acc_ref[...]
    return _layer_norm(z, g_ref[...], b_ref[...])


def _ffn_ln_kernel(x_ref, wgu_ref, wd_ref, g_ref, b_ref, o_ref, acc_ref):
    o_ref[...] = _ffn_ln_value(x_ref[...], wgu_ref, wd_ref, g_ref, b_ref, acc_ref)


def _ffn_ln(x2, w_gu, w_down_half, ln_g, ln_b, l, j, ln_idx):
    n = x2.shape[0]
    tm = TOKEN_TILE
    return pl.pallas_call(
        _ffn_ln_kernel,
        grid=(n // tm,),
        in_specs=[pl.BlockSpec((tm, D_MODEL), lambda i: (i, 0)),
                  _picked(w_gu, (l, j)), _picked(w_down_half, (l, j)),
                  _picked(ln_g, (l, ln_idx)), _picked(ln_b, (l, ln_idx))],
        out_specs=pl.BlockSpec((tm, D_MODEL), lambda i: (i, 0)),
        out_shape=jax.ShapeDtypeStruct((n, D_MODEL), F32),
        scratch_shapes=[pltpu.VMEM((tm, D_MODEL), F32)],
        compiler_params=_params(("arbitrary",)),
        name="ffn_ln",
    )(x2, w_gu, w_down_half, ln_g, ln_b)


def _swap_halves_lanes(t, first_half):
    fwd = pltpu.roll(t, A_DH // 2, axis=1)
    bwd = pltpu.roll(t, LANES - A_DH // 2, axis=1)
    return jnp.where(first_half, bwd, fwd)


def _swap_halves_rows(t):
    half = A_DH // 2
    return jnp.concatenate([t[half:2 * half], t[:half], t[3 * half:], t[2 * half:3 * half]], axis=0)


def _sgu_tile(u_ref, vn_ref, w_ref, bs_ref, o_ref):
    tm = u_ref.shape[0]
    t_row = lax.broadcasted_iota(jnp.int32, (SGU_CHUNK, SGU_CHUNK), 0)
    t_col = lax.broadcasted_iota(jnp.int32, (SGU_CHUNK, SGU_CHUNK), 1)
    causal = t_col <= t_row
    lane = lax.broadcasted_iota(jnp.int32, (SGU_CHUNK, LANES), 1)
    lo_mask = lane < B_DG
    for pair in range(B_WIDTH // LANES):
        w_lo = jnp.where(causal, w_ref[2 * pair], 0.0).astype(BF16)
        w_hi = jnp.where(causal, w_ref[2 * pair + 1], 0.0).astype(BF16)
        cols = slice(pair * LANES, (pair + 1) * LANES)
        bias = bs_ref[:, cols]
        for c in range(tm // SGU_CHUNK):
            rows = slice(c * SGU_CHUNK, (c + 1) * SGU_CHUNK)
            vp = vn_ref[rows, cols]
            zero = jnp.zeros_like(vp)
            mixed = _dot(w_lo, jnp.where(lo_mask, vp, zero)) + _dot(w_hi, jnp.where(lo_mask, zero, vp))
            o_ref[0, rows, cols] = (u_ref[rows, cols].astype(F32) * (mixed + bias)).astype(BF16)


def _proj_ab_kernel(x_ref, w_ref, wqvt_ref, cos_ref, sin_ref, cost_ref, sint_ref, lng_ref, lnb_ref,
                    sw_ref, sb_ref, qt_ref, k_ref, vt_ref, bg_ref, u_ref, vn_ref):
    xb = x_ref[0].astype(BF16)
    cos = cos_ref[...]
    sin = sin_ref[...]
    cos_t = cost_ref[...]
    sin_t = sint_ref[...]
    lane = lax.broadcasted_iota(jnp.int32, cos.shape, 1)
    first_half = (lane % A_DH) < (A_DH // 2)
    q_t = _dot_nt(wqvt_ref[:A_WIDTH, :], xb)
    for grp in range(A_WIDTH // LANES):
        lo = grp * LANES
        qg = q_t[lo:lo + LANES]
        qg = qg * cos_t + _swap_halves_rows(qg) * sin_t
        qt_ref[0, lo:lo + LANES, :] = (qg * (A_DH ** -0.5 * LOG2E)).astype(BF16)
        k = _dot(xb, w_ref[:, lo:lo + LANES])
        k = k * cos + _swap_halves_lanes(k, first_half) * sin
        k_ref[0, :, lo:lo + LANES] = k.astype(BF16)
    vt_ref[0] = _dot_nt(wqvt_ref[A_WIDTH:, :], xb).astype(BF16)
    ub = _dot(xb, w_ref[:, A_WIDTH:A_WIDTH + B_WIDTH])
    u_ref[...] = jax.nn.gelu(ub).astype(BF16)
    vb = _dot(xb, w_ref[:, A_WIDTH + B_WIDTH:])
    vn_ref[...] = _layer_norm(jax.nn.gelu(vb), lng_ref[...], lnb_ref[...]).astype(BF16)
    _sgu_tile(u_ref, vn_ref, sw_ref, sb_ref, bg_ref)


def _proj_ab(x, w_kuv, w_qv_t, cos, sin, cos_t, sin_t, lng, lnb, sgu_w, bs_full):
    bn, s, _ = x.shape
    tm = min(s, PROJ_TILE)
    tok = lambda b, t: (b, t, 0)
    tok_t = lambda b, t: (b, 0, t)
    out = jax.ShapeDtypeStruct((bn, s, A_WIDTH), BF16)
    out_t = jax.ShapeDtypeStruct((bn, A_WIDTH, s), BF16)
    return pl.pallas_call(
        _proj_ab_kernel,
        grid=(bn, s // tm),
        in_specs=[
            pl.BlockSpec((1, tm, D_MODEL), tok),
            _resident(w_kuv.shape),
            _resident(w_qv_t.shape),
            pl.BlockSpec((tm, LANES), lambda b, t: (t, 0)),
            pl.BlockSpec((tm, LANES), lambda b, t: (t, 0)),
            pl.BlockSpec((LANES, tm), lambda b, t: (0, t)),
            pl.BlockSpec((LANES, tm), lambda b, t: (0, t)),
            _resident(lng.shape),
            _resident(lnb.shape),
            _resident(sgu_w.shape),
            _resident(bs_full.shape),
        ],
        out_specs=[pl.BlockSpec((1, A_WIDTH, tm), tok_t), pl.BlockSpec((1, tm, A_WIDTH), tok),
                   pl.BlockSpec((1, A_WIDTH, tm), tok_t), pl.BlockSpec((1, tm, B_WIDTH), tok)],
        out_shape=[out_t, out, out_t, out],
        scratch_shapes=[pltpu.VMEM((tm, B_WIDTH), BF16)] * 2,
        compiler_params=_params(("arbitrary", "arbitrary")),
        name="proj_ab",
    )(x, w_kuv, w_qv_t, cos, sin, cos_t, sin_t, lng, lnb, sgu_w, bs_full)


def _moba_gate_kernel(qt_ref, k_ref, avg_ref, bias_ref):
    blk = MOBA_BLOCK
    gate_rows = avg_ref.shape[0]
    s_len = qt_ref.shape[2]
    width = min(s_len, MOBA_GATE_WIDTH)
    kmean = _dot(avg_ref[...], k_ref[0])
    lane_g = lax.broadcasted_iota(jnp.int32, kmean.shape, 1)
    blk_row = lax.broadcasted_iota(jnp.int32, (gate_rows, width), 0)
    feat_row = lax.broadcasted_iota(jnp.int32, (LANES, width), 0)
    qry_lane = lax.broadcasted_iota(jnp.int32, (1, width), 1)
    for c in range(s_len // width):
        q_pair = qt_ref[0, :, c * width:(c + 1) * width]
        q_blk = (qry_lane + c * width) // blk
        for h in range(2):
            head_rows = (feat_row < A_DH) if h == 0 else (feat_row >= A_DH)
            head_lanes = (lane_g < A_DH) if h == 0 else (lane_g >= A_DH)
            qh = jnp.where(head_rows, q_pair, jnp.zeros_like(q_pair))
            gate = _dot(jnp.where(head_lanes, kmean, 0.0).astype(BF16), qh)
            gate = jnp.where(blk_row < q_blk, gate, NEG)
            chosen = jnp.zeros((gate_rows, width), jnp.bool_)
            for _ in range(MOBA_TOPK):
                best = jnp.max(gate, axis=0, keepdims=True)
                idx = jnp.min(jnp.where(gate == best, blk_row, LANES), axis=0, keepdims=True)
                pick = blk_row == idx
                chosen = chosen | (pick & (idx < q_blk))
                gate = jnp.where(pick, -jnp.inf, gate)
            bias_ref[0, h, :, c * width:(c + 1) * width] = jnp.where(chosen, 0.0, NEG).astype(BF16)


def _moba_gate(q_t, k):
    bn, s, _ = k.shape
    blk = MOBA_BLOCK
    nb = s // blk
    assert nb <= LANES
    gate_rows = -(-nb // 16) * 16
    block_avg = ((jnp.arange(gate_rows)[:, None] == jnp.arange(s)[None, :] // blk) / blk).astype(BF16)
    return pl.pallas_call(
        _moba_gate_kernel,
        grid=(bn, A_WIDTH // LANES),
        in_specs=[
            pl.BlockSpec((1, LANES, s), lambda b, p: (b, p, 0)),
            pl.BlockSpec((1, s, LANES), lambda b, p: (b, 0, p)),
            _resident((gate_rows, s)),
        ],
        out_specs=pl.BlockSpec((1, 2, gate_rows, s), lambda b, p: (b, p, 0, 0)),
        out_shape=jax.ShapeDtypeStruct((bn, A_HEADS, gate_rows, s), BF16),
        compiler_params=_params(("arbitrary", "arbitrary")),
        name="moba_gate",
    )(q_t, k, block_avg)


def _moba_kernel(qt_ref, k_ref, vt_ref, bias_ref, e_ref, o_ref, *ring):
    i = pl.program_id(2)
    blk = MOBA_BLOCK
    nb = k_ref.shape[1] // blk
    pairs = qt_ref.shape[1] // LANES
    depth = len(ring) // pairs
    ahead = depth - 2
    gate_rows = bias_ref.shape[2]
    feat_row = lax.broadcasted_iota(jnp.int32, (LANES, blk), 0)
    key_idx = lax.broadcasted_iota(jnp.int32, (blk, blk), 0)
    qry_idx = lax.broadcasted_iota(jnp.int32, (blk, blk), 1)
    causal = key_idx <= qry_idx
    bias_pad = jnp.zeros((LANES - gate_rows, blk), BF16)
    ones_rows = jnp.ones((MOBA_SUM_ROWS, blk), BF16)
    own = pl.multiple_of(i * blk, blk)

    def key_start(j):
        return pl.multiple_of(jnp.minimum(j, nb - 1) * blk, blk)

    def values(head, start):
        return jnp.concatenate([vt_ref[0, head * A_DH:(head + 1) * A_DH, pl.ds(start, blk)], ones_rows], axis=0)

    def absorb(pair, s_both, start, carry, mask=None):
        out = []
        for h in range(2):
            s = s_both[:, h * blk:(h + 1) * blk]
            if mask is not None:
                s = jnp.where(mask, s, NEG)
            m_new = jnp.max(s, axis=0, keepdims=True)
            pv = lambda p: _dot(values(2 * pair + h, start), p.astype(BF16))
            if carry is None:
                out += [m_new, pv(jnp.exp2(s - m_new))]
            else:
                m_old, a_old = carry[2 * h], carry[2 * h + 1]
                m_new = jnp.maximum(m_old, m_new)
                out += [m_new, jnp.exp2(m_old - m_new) * a_old + pv(jnp.exp2(s - m_new))]
        return out

    rhs, carry = [], []
    for pair in range(pairs):
        rows = slice(pair * LANES, (pair + 1) * LANES)
        q_pair = qt_ref[0, rows, :]
        qh = [jnp.where(feat_row < A_DH, q_pair, jnp.zeros_like(q_pair)),
              jnp.where(feat_row >= A_DH, q_pair, jnp.zeros_like(q_pair))]
        rhs.append(jnp.concatenate([jnp.concatenate([qh[h], bias_ref[0, 2 * pair + h], bias_pad], axis=0)
                                    for h in range(2)], axis=1))
        s_own = _dot(k_ref[0, pl.ds(own, blk), rows], jnp.concatenate(qh, axis=1))
        carry += absorb(pair, s_own, own, None, causal)

    def scores(pair, j):
        start = key_start(j)
        lhs = jnp.concatenate([k_ref[0, pl.ds(start, blk), pair * LANES:(pair + 1) * LANES],
                               e_ref[pl.ds(start, blk), :]], axis=1)
        return _dot(lhs, rhs[pair])

    for pair in range(pairs):
        for r in range(ahead):
            ring[pair * depth + r][...] = scores(pair, r)

    def body(it, carry):
        carry = list(carry)
        for r in range(depth):
            j = depth * it + r
            for pair in range(pairs):
                ring[pair * depth + (r + ahead) % depth][...] = scores(pair, j + ahead)
                carry[4 * pair:4 * pair + 4] = absorb(pair, ring[pair * depth + r][...], key_start(j),
                                                      carry[4 * pair:4 * pair + 4])
        return tuple(carry)

    carry = lax.fori_loop(0, (i + depth - 1) // depth, body, tuple(carry))
    out_t = jnp.concatenate([acc[:A_DH] / acc[A_DH:A_DH + 1] for acc in carry[1::2]], axis=0)
    o_ref[0] = out_t.T.astype(BF16)


def _moba(q_t, k, v_t, bias):
    bn, s, _ = k.shape
    blk = MOBA_BLOCK
    nb = s // blk
    gate_rows = bias.shape[2]
    width = MOBA_PAIRS * LANES
    block_onehot = (jnp.arange(s)[:, None] // blk == jnp.arange(LANES)[None, :]).astype(BF16)
    return pl.pallas_call(
        _moba_kernel,
        grid=(bn, A_WIDTH // width, nb),
        in_specs=[
            pl.BlockSpec((1, width, blk), lambda b, p, i: (b, p, i)),
            pl.BlockSpec((1, s, width), lambda b, p, i: (b, 0, p)),
            pl.BlockSpec((1, width, s), lambda b, p, i: (b, p, 0)),
            pl.BlockSpec((1, 2 * MOBA_PAIRS, gate_rows, blk), lambda b, p, i: (b, p, 0, i)),
            _resident((s, LANES)),
        ],
        out_specs=pl.BlockSpec((1, blk, width), lambda b, p, i: (b, i, p)),
        out_shape=jax.ShapeDtypeStruct((bn, s, A_WIDTH), BF16),
        scratch_shapes=[pltpu.VMEM((blk, 2 * blk), F32)] * (MOBA_PAIRS * MOBA_RING),
        compiler_params=_params(("arbitrary", "arbitrary", "arbitrary")),
        name="moba",
    )(q_t, k, v_t, bias, block_onehot)


def _out_ffn_kernel(n_parts, *refs):
    x_ref = refs[0]
    part_refs = refs[1:1 + n_parts]
    w_ref, g1_ref, b1_ref, wgu_ref, wd_ref, g2_ref, b2_ref, o_ref, acc_ref = refs[1 + n_parts:]
    y = None
    lo = 0
    for p_ref in part_refs:
        width = p_ref.shape[-1]
        term = _dot(p_ref[...], w_ref[lo:lo + width, :])
        y = term if y is None else y + term
        lo += width
    mid = _layer_norm(ALPHA * x_ref[...] + y, g1_ref[...], b1_ref[...])
    o_ref[...] = _ffn_ln_value(mid, wgu_ref, wd_ref, g2_ref, b2_ref, acc_ref)


def _out_ffn(x2, parts, w_out, w_gu, w_down_half, ln_g, ln_b, l):
    n = x2.shape[0]
    tm = TOKEN_TILE
    tok = lambda i: (i, 0)
    return pl.pallas_call(
        functools.partial(_out_ffn_kernel, len(parts)),
        grid=(n // tm,),
        in_specs=[pl.BlockSpec((tm, D_MODEL), tok)]
        + [pl.BlockSpec((tm, p.shape[-1]), tok) for p in parts]
        + [_resident(w_out.shape), _picked(ln_g, (l, 1)), _picked(ln_b, (l, 1)),
           _picked(w_gu, (l, 1)), _picked(w_down_half, (l, 1)), _picked(ln_g, (l, 2)), _picked(ln_b, (l, 2))],
        out_specs=pl.BlockSpec((tm, D_MODEL), tok),
        out_shape=jax.ShapeDtypeStruct((n, D_MODEL), F32),
        scratch_shapes=[pltpu.VMEM((tm, D_MODEL), F32)],
        compiler_params=_params(("arbitrary",)),
        name="out_ffn",
    )(x2, *parts, w_out, ln_g, ln_b, w_gu, w_down_half, ln_g, ln_b)


def _proj_c_kernel(x_ref, wqk_ref, wvt_ref, wot_ref, wif_ref, wift_ref, cw_ref, cb_ref, gb_ref, gbt_ref,
                   q_ref, k_ref, vt_ref, ogt_ref, gates_ref, gatest_ref, *pre_refs):
    t = pl.program_id(1)
    tm = x_ref.shape[1]
    xb = x_ref[0].astype(BF16)
    width = pre_refs[0].shape[1]

    @pl.when(t == 0)
    def _():
        for pre_ref in pre_refs:
            pre_ref[:SUBLANES, :] = jnp.zeros((SUBLANES, width), F32)

    gates_ref[0] = _dot(xb, wif_ref[...]) + gb_ref[...]
    gatest_ref[0] = _dot_nt(wift_ref[...], xb) + gbt_ref[...]
    for c, pre_ref in enumerate(pre_refs):
        pre_ref[SUBLANES:, :] = _dot(xb, wqk_ref[:, c * width:(c + 1) * width])
    for c, pre_ref in enumerate(pre_refs):
        cols = slice(c * width, (c + 1) * width)
        vt_ref[0, cols, :] = _dot_nt(wvt_ref[cols, :], xb).astype(BF16)
        conv = cb_ref[:, cols] + cw_ref[CONV_W - 1:CONV_W, cols] * pre_ref[SUBLANES:, :]
        for d in range(1, CONV_W):
            conv = conv + cw_ref[CONV_W - 1 - d:CONV_W - d, cols] * pre_ref[SUBLANES - d:SUBLANES - d + tm, :]
        pre_ref[:SUBLANES, :] = pre_ref[tm:, :]
        act = jax.nn.silu(conv)
        if c * width < C_QK_WIDTH:
            q_ref[0, :, cols] = act.astype(BF16)
        else:
            k_cols = slice(c * width - C_QK_WIDTH, (c + 1) * width - C_QK_WIDTH)
            k_ref[0, :, k_cols] = (act * (C_DQK ** -0.5)).astype(BF16)
        ogt_ref[0, cols, :] = jax.nn.sigmoid(_dot_nt(wot_ref[cols, :], xb)).astype(BF16)


def _proj_c(x, wqk, wv_t, wo_t, wif, wift, conv_w, conv_b, gb, gbt):
    bn, s, _ = x.shape
    tm = min(s, PROJ_TILE)
    tok = lambda b, t: (b, t, 0)
    return pl.pallas_call(
        _proj_c_kernel,
        grid=(bn, s // tm),
        in_specs=[pl.BlockSpec((1, tm, D_MODEL), tok)]
        + [_resident(a.shape) for a in (wqk, wv_t, wo_t, wif, wift, conv_w, conv_b, gb, gbt)],
        out_specs=[
            pl.BlockSpec((1, tm, C_QK_WIDTH), tok),
            pl.BlockSpec((1, tm, C_QK_WIDTH), tok),
            pl.BlockSpec((1, C_WIDTH, tm), lambda b, t: (b, 0, t)),
            pl.BlockSpec((1, C_WIDTH, tm), lambda b, t: (b, 0, t)),
            pl.BlockSpec((1, tm, LANES), tok),
            pl.BlockSpec((1, SUBLANES, tm), lambda b, t: (b, 0, t)),
        ],
        out_shape=[
            jax.ShapeDtypeStruct((bn, s, C_QK_WIDTH), BF16),
            jax.ShapeDtypeStruct((bn, s, C_QK_WIDTH), BF16),
            jax.ShapeDtypeStruct((bn, C_WIDTH, s), BF16),
            jax.ShapeDtypeStruct((bn, C_WIDTH, s), BF16),
            jax.ShapeDtypeStruct((bn, s, LANES), F32),
            jax.ShapeDtypeStruct((bn, SUBLANES, s), F32),
        ],
        scratch_shapes=[pltpu.VMEM((SUBLANES + tm, PROJ_CHUNK), F32)] * (2 * C_QK_WIDTH // PROJ_CHUNK),
        compiler_params=_params(("arbitrary", "arbitrary")),
        name="proj_c",
    )(x, wqk, wv_t, wo_t, wif, wift, conv_w, conv_b, gb, gbt)


def _split3(a):
    hi = a.astype(BF16)
    r1 = a - hi.astype(F32)
    mid = r1.astype(BF16)
    lo = (r1 - mid.astype(F32)).astype(BF16)
    return hi, mid, lo


def _mlstm_kernel(q_ref, k_ref, vt_ref, ogt_ref, gates_ref, gatest_ref, hg_ref, o_ref,
                  ct_ref, n_ref, m_ref):
    c_idx = pl.program_id(1)
    L = q_ref.shape[1]

    @pl.when(c_idx == 0)
    def _():
        ct_ref[...] = jnp.zeros_like(ct_ref)
        n_ref[...] = jnp.zeros_like(n_ref)
        m_ref[...] = jnp.zeros_like(m_ref)

    src_idx = lax.broadcasted_iota(jnp.int32, (L, L), 0)
    tgt_idx = lax.broadcasted_iota(jnp.int32, (L, L), 1)
    causal = src_idx <= tgt_idx
    tri = (tgt_idx <= src_idx).astype(BF16)
    tri_t = causal.astype(BF16)

    gates = gates_ref[0]
    gates_t = gatest_ref[0]
    b_cols = sum(_dot(tri, part) for part in _split3(jax.nn.log_sigmoid(gates)))
    b_rows = sum(_dot(part, tri_t) for part in _split3(jax.nn.log_sigmoid(gates_t)))

    for h in range(C_HEADS):
        qk_cols = slice(h * C_DQK, (h + 1) * C_DQK)
        v_rows = slice(h * C_DV, (h + 1) * C_DV)
        c_col = gates[:, h:h + 1] - b_cols[:, C_HEADS + h:C_HEADS + h + 1]
        i_row = gates_t[h:h + 1, :]
        b_row = b_rows[C_HEADS + h:C_HEADS + h + 1, :]
        b_end = b_row[:, L - 1:L]
        m_in = m_ref[h][:, 0:1]
        ct_in = ct_ref[h]
        n_in = n_ref[h]

        masked_c = jnp.where(causal, c_col, -jnp.inf)
        m_row = jnp.maximum(m_in, jnp.max(masked_c, axis=0, keepdims=True))
        p_t = jnp.exp(masked_c - m_row)
        inter = jnp.exp(m_in - m_row)

        q_t = q_ref[0, :, qk_cols].astype(F32).T.astype(BF16)
        kh = k_ref[0, :, qk_cols]
        vt_h = vt_ref[0, v_rows, :]
        sqk_t = _dot(kh, q_t) * p_t
        num_t = _dot(vt_h, sqk_t.astype(BF16)) + inter * _dot(ct_in.astype(BF16), q_t)
        n_q = _dot(jnp.broadcast_to(n_in, (SUBLANES, C_DQK)).astype(BF16), q_t)[0:1]
        den = jnp.sum(sqk_t, axis=0, keepdims=True) + inter * n_q
        hid_t = num_t * (1.0 / jnp.maximum(jnp.abs(den), jnp.exp(-(b_row + m_row))))
        mu = jnp.mean(hid_t, axis=0, keepdims=True)
        zc = hid_t - mu
        var = jnp.mean(zc * zc, axis=0, keepdims=True)
        out_t = ogt_ref[0, v_rows, :].astype(F32) * (zc * lax.rsqrt(var + LN_EPS) * hg_ref[v_rows, :])
        o_ref[0, :, v_rows] = out_t.T.astype(BF16)

        a_row = b_end - b_row + i_row
        a_max = jnp.max(a_row, axis=1, keepdims=True)
        w_end = jnp.exp(a_row - a_max)
        m_new = jnp.maximum(b_end + m_in, a_max)
        decay = jnp.exp(b_end + m_in - m_new)
        inject = jnp.exp(a_max - m_new)
        vw = (vt_h.astype(F32) * w_end).astype(BF16)
        ct_ref[h] = decay * ct_in + inject * _dot(vw, kh)
        k_sum = _dot(jnp.broadcast_to(w_end, (SUBLANES, L)).astype(BF16), kh)[0:1]
        n_ref[h] = decay * n_in + inject * k_sum
        m_ref[h] = jnp.broadcast_to(m_new, (1, LANES))


def _mlstm(q, k, v_t, og_t, gates, gates_t, head_g):
    bn, s, _ = q.shape
    L = MLSTM_L
    tok = lambda b, c: (b, c, 0)
    feat = lambda b, c: (b, 0, c)
    hg_cols = jnp.broadcast_to(head_g.reshape(C_WIDTH, 1), (C_WIDTH, L))
    return pl.pallas_call(
        _mlstm_kernel,
        grid=(bn, s // L),
        in_specs=[
            pl.BlockSpec((1, L, C_QK_WIDTH), tok),
            pl.BlockSpec((1, L, C_QK_WIDTH), tok),
            pl.BlockSpec((1, C_WIDTH, L), feat),
            pl.BlockSpec((1, C_WIDTH, L), feat),
            pl.BlockSpec((1, L, LANES), tok),
            pl.BlockSpec((1, SUBLANES, L), feat),
            _resident((C_WIDTH, L)),
        ],
        out_specs=pl.BlockSpec((1, L, C_WIDTH), tok),
        out_shape=jax.ShapeDtypeStruct((bn, s, C_WIDTH), BF16),
        scratch_shapes=[
            pltpu.VMEM((C_HEADS, C_DV, C_DQK), F32),
            pltpu.VMEM((C_HEADS, 1, C_DQK), F32),
            pltpu.VMEM((C_HEADS, 1, LANES), F32),
        ],
        compiler_params=_params(("arbitrary", "arbitrary")),
        name="mlstm",
    )(q, k, v_t, og_t, gates, gates_t, hg_cols)


def _rope_tables(s):
    half = A_DH // 2
    inv = ROPE_THETA ** (-jnp.arange(half, dtype=F32) / half)
    ang = jnp.arange(s).astype(F32)[:, None] * inv[None, :]
    cos, sin = jnp.cos(ang), jnp.sin(ang)
    cos = jnp.tile(cos, (1, LANES // half))
    sin = jnp.tile(jnp.concatenate([-sin, sin], axis=1), (1, LANES // A_DH))
    return cos, sin, cos.T, sin.T


def _mixer_ab(x, w_in, sgu_ln_g, sgu_ln_b, sgu_w, sgu_b):
    bn, s, d = x.shape
    cos, sin, cos_t, sin_t = _rope_tables(s)
    w_kuv = jnp.concatenate([w_in[:, A_WIDTH:2 * A_WIDTH], w_in[:, 3 * A_WIDTH:]], axis=1).astype(BF16)
    w_qv_t = jnp.concatenate([w_in[:, :A_WIDTH], w_in[:, 2 * A_WIDTH:3 * A_WIDTH]], axis=1).T.astype(BF16)
    bs_full = jnp.repeat(sgu_b.T, B_DG, axis=1)
    q_t, k, v_t, bg = _proj_ab(x, w_kuv, w_qv_t, cos, sin, cos_t, sin_t,
                               sgu_ln_g.reshape(1, B_WIDTH), sgu_ln_b.reshape(1, B_WIDTH), sgu_w, bs_full)
    a = _moba(q_t, k, v_t, _moba_gate(q_t, k))
    return [a.reshape(bn * s, A_WIDTH), bg.reshape(bn * s, B_WIDTH)]


def _mixer_c(x, w_in, conv_w, conv_b, b_i, b_f, head_g):
    bn, s, d = x.shape
    qk_w = 2 * C_QK_WIDTH
    wqk = w_in[:, :qk_w].astype(BF16)
    wv_t = w_in[:, qk_w:qk_w + C_WIDTH].T.astype(BF16)
    wo_t = w_in[:, qk_w + C_WIDTH:qk_w + 2 * C_WIDTH].T.astype(BF16)
    w_if = w_in[:, qk_w + 2 * C_WIDTH:]
    wif = jnp.pad(w_if, ((0, 0), (0, LANES - 2 * C_HEADS))).astype(BF16)
    wift = w_if.T.astype(BF16)
    gate_bias = jnp.concatenate([b_i, b_f]).astype(F32)
    gb = jnp.pad(gate_bias, (0, LANES - 2 * C_HEADS)).reshape(1, LANES)
    gbt = gate_bias.reshape(2 * C_HEADS, 1)
    q, k, v_t, og_t, gates, gates_t = _proj_c(x, wqk, wv_t, wo_t, wif, wift, conv_w,
                                              conv_b.reshape(1, qk_w), gb, gbt)
    hg = _mlstm(q, k, v_t, og_t, gates, gates_t, head_g)
    return [hg.reshape(bn * s, C_WIDTH)]


def kernel(x, ln_g, ln_b, ffn_w_gu, ffn_w_down, ab_w_in, sgu_ln_g, sgu_ln_b, sgu_w, sgu_b, ab_w_out,
           c_w_in, c_conv_w, c_conv_b, c_b_i, c_b_f, c_head_g, c_w_out):
    bn, s, d = x.shape
    n = bn * s

    w_gu = ffn_w_gu.astype(BF16)
    w_down_half = (0.5 * ffn_w_down).astype(BF16)
    ln_g = ln_g.reshape(DEPTH, 3, 1, d)
    ln_b = ln_b.reshape(DEPTH, 3, 1, d)
    x = x.reshape(n, d)
    for l in range(DEPTH):
        x = _ffn_ln(x, w_gu, w_down_half, ln_g, ln_b, l, 0, 0)
        j = l // 2
        if l % 2 == 0:
            parts = _mixer_ab(x.reshape(bn, s, d), ab_w_in[j], sgu_ln_g[j], sgu_ln_b[j], sgu_w[j], sgu_b[j])
            w_out = ab_w_out[j]
        else:
            parts = _mixer_c(x.reshape(bn, s, d), c_w_in[j], c_conv_w[j], c_conv_b[j], c_b_i[j], c_b_f[j],
                             c_head_g[j])
            w_out = c_w_out[j]
        x = _out_ffn(x, parts, w_out.astype(BF16), w_gu, w_down_half, ln_g, ln_b, l)
    return x.reshape(bn, s, d)
```

```python
import functools

import jax
import jax.numpy as jnp
from jax import lax
from jax.experimental import pallas as pl
from jax.experimental.pallas import tpu as pltpu

D_MODEL = 1024
DEPTH = 2
D_FF = 2816
ALPHA = (2 * DEPTH) ** 0.25
LN_EPS = 1e-5
NEG = -1e30

A_HEADS = 8
A_DH = 64
A_WIDTH = A_HEADS * A_DH
MOBA_BLOCK = 256
MOBA_TOPK = 3
ROPE_THETA = 10000.0

B_GROUPS = 8
B_DG = 64
B_WIDTH = B_GROUPS * B_DG
SGU_CHUNK = 128

C_HEADS = 4
C_DQK = 128
C_DV = 256
C_QK_WIDTH = C_HEADS * C_DQK
C_WIDTH = C_HEADS * C_DV
CONV_W = 4

LANES = 128
SUBLANES = 8
VMEM_LIMIT = 56 * 1024 * 1024

TOKEN_TILE = 512
PROJ_TILE = 1024
FF_CHUNK = 256
PROJ_CHUNK = 256
MLSTM_L = 256
MOBA_RING = 3
MOBA_PAIRS = 4
MOBA_SUM_ROWS = 16
MOBA_GATE_WIDTH = 1024
LOG2E = 1.4426950408889634

BF16 = jnp.bfloat16
F32 = jnp.float32


def _params(sem):
    return pltpu.CompilerParams(dimension_semantics=sem, vmem_limit_bytes=VMEM_LIMIT)


def _dot(a, b):
    return jnp.dot(a, b, preferred_element_type=F32)


def _dot_nt(a, b):
    return lax.dot_general(a, b, (((1,), (1,)), ((), ())), preferred_element_type=F32)


def _layer_norm(z, g, b):
    mu = jnp.mean(z, axis=-1, keepdims=True)
    zc = z - mu
    var = jnp.mean(zc * zc, axis=-1, keepdims=True)
    y = zc * lax.rsqrt(var + LN_EPS) * g
    return y if b is None else y + b


def _resident(shape):
    return pl.BlockSpec(shape, lambda *_: (0,) * len(shape), pipeline_mode=pl.Buffered(1))


def _picked(arr, lead):
    rest = arr.shape[len(lead):]
    return pl.BlockSpec((None,) * len(lead) + rest, lambda *_: tuple(lead) + (0,) * len(rest),
                        pipeline_mode=pl.Buffered(1))


def _ffn_ln_value(x, wgu_ref, wd_ref, g_ref, b_ref, acc_ref):
    xb = x.astype(BF16)
    for c in range(D_FF // FF_CHUNK):
        lo = c * FF_CHUNK
        gate = _dot(xb, wgu_ref[:, lo:lo + FF_CHUNK])
        up = _dot(xb, wgu_ref[:, D_FF + lo:D_FF + lo + FF_CHUNK])
        h = (jax.nn.silu(gate) * up).astype(BF16)
        part = _dot(h, wd_ref[lo:lo + FF_CHUNK, :])
        if c == 0:
            acc_ref[...] = part
        else:
            acc_ref[...] += part
    z = ALPHA * x + acc_ref[...]
    return _layer_norm(z, g_ref[...], b_ref[...])


def _ffn_ln_kernel(x_ref, wgu_ref, wd_ref, g_ref, b_ref, o_ref, acc_ref):
    o_ref[...] = _ffn_ln_value(x_ref[...], wgu_ref, wd_ref, g_ref, b_ref, acc_ref)


def _ffn_ln(x2, w_gu, w_down_half, ln_g, ln_b, l, j, ln_idx):
    n = x2.shape[0]
    tm = TOKEN_TILE
    return pl.pallas_call(
        _ffn_ln_kernel,
        grid=(n // tm,),
        in_specs=[pl.BlockSpec((tm, D_MODEL), lambda i: (i, 0)),
                  _picked(w_gu, (l, j)), _picked(w_down_half, (l, j)),
                  _picked(ln_g, (l, ln_idx)), _picked(ln_b, (l, ln_idx))],
        out_specs=pl.BlockSpec((tm, D_MODEL), lambda i: (i, 0)),
        out_shape=jax.ShapeDtypeStruct((n, D_MODEL), F32),
        scratch_shapes=[pltpu.VMEM((tm, D_MODEL), F32)],
        compiler_params=_params(("arbitrary",)),
        name="ffn_ln",
    )(x2, w_gu, w_down_half, ln_g, ln_b)


def _swap_halves_lanes(t, first_half):
    fwd = pltpu.roll(t, A_DH // 2, axis=1)
    bwd = pltpu.roll(t, LANES - A_DH // 2, axis=1)
    return jnp.where(first_half, bwd, fwd)


def _swap_halves_rows(t):
    half = A_DH // 2
    return jnp.concatenate([t[half:2 * half], t[:half], t[3 * half:], t[2 * half:3 * half]], axis=0)


def _sgu_tile(u_ref, vn_ref, w_ref, bs_ref, o_ref):
    tm = u_ref.shape[0]
    t_row = lax.broadcasted_iota(jnp.int32, (SGU_CHUNK, SGU_CHUNK), 0)
    t_col = lax.broadcasted_iota(jnp.int32, (SGU_CHUNK, SGU_CHUNK), 1)
    causal = t_col <= t_row
    lane = lax.broadcasted_iota(jnp.int32, (SGU_CHUNK, LANES), 1)
    lo_mask = lane < B_DG
    for pair in range(B_WIDTH // LANES):
        w_lo = jnp.where(causal, w_ref[2 * pair], 0.0).astype(BF16)
        w_hi = jnp.where(causal, w_ref[2 * pair + 1], 0.0).astype(BF16)
        cols = slice(pair * LANES, (pair + 1) * LANES)
        bias = bs_ref[:, cols]
        for c in range(tm // SGU_CHUNK):
            rows = slice(c * SGU_CHUNK, (c + 1) * SGU_CHUNK)
            vp = vn_ref[rows, cols]
            zero = jnp.zeros_like(vp)
            mixed = _dot(w_lo, jnp.where(lo_mask, vp, zero)) + _dot(w_hi, jnp.where(lo_mask, zero, vp))
            o_ref[0, rows, cols] = (u_ref[rows, cols].astype(F32) * (mixed + bias)).astype(BF16)


def _proj_ab_kernel(x_ref, w_ref, wqvt_ref, cos_ref, sin_ref, cost_ref, sint_ref, lng_ref, lnb_ref,
                    sw_ref, sb_ref, qt_ref, k_ref, vt_ref, bg_ref, u_ref, vn_ref):
    xb = x_ref[0].astype(BF16)
    cos = cos_ref[...]
    sin = sin_ref[...]
    cos_t = cost_ref[...]
    sin_t = sint_ref[...]
    lane = lax.broadcasted_iota(jnp.int32, cos.shape, 1)
    first_half = (lane % A_DH) < (A_DH // 2)
    q_t = _dot_nt(wqvt_ref[:A_WIDTH, :], xb)
    for grp in range(A_WIDTH // LANES):
        lo = grp * LANES
        qg = q_t[lo:lo + LANES]
        qg = qg * cos_t + _swap_halves_rows(qg) * sin_t
        qt_ref[0, lo:lo + LANES, :] = (qg * (A_DH ** -0.5 * LOG2E)).astype(BF16)
        k = _dot(xb, w_ref[:, lo:lo + LANES])
        k = k * cos + _swap_halves_lanes(k, first_half) * sin
        k_ref[0, :, lo:lo + LANES] = k.astype(BF16)
    vt_ref[0] = _dot_nt(wqvt_ref[A_WIDTH:, :], xb).astype(BF16)
    ub = _dot(xb, w_ref[:, A_WIDTH:A_WIDTH + B_WIDTH])
    u_ref[...] = jax.nn.gelu(ub).astype(BF16)
    vb = _dot(xb, w_ref[:, A_WIDTH + B_WIDTH:])
    vn_ref[...] = _layer_norm(jax.nn.gelu(vb), lng_ref[...], lnb_ref[...]).astype(BF16)
    _sgu_tile(u_ref, vn_ref, sw_ref, sb_ref, bg_ref)


def _proj_ab(x, w_kuv, w_qv_t, cos, sin, cos_t, sin_t, lng, lnb, sgu_w, bs_full):
    bn, s, _ = x.shape
    tm = min(s, PROJ_TILE)
    tok = lambda b, t: (b, t, 0)
    tok_t = lambda b, t: (b, 0, t)
    out = jax.ShapeDtypeStruct((bn, s, A_WIDTH), BF16)
    out_t = jax.ShapeDtypeStruct((bn, A_WIDTH, s), BF16)
    return pl.pallas_call(
        _proj_ab_kernel,
        grid=(bn, s // tm),
        in_specs=[
            pl.BlockSpec((1, tm, D_MODEL), tok),
            _resident(w_kuv.shape),
            _resident(w_qv_t.shape),
            pl.BlockSpec((tm, LANES), lambda b, t: (t, 0)),
            pl.BlockSpec((tm, LANES), lambda b, t: (t, 0)),
            pl.BlockSpec((LANES, tm), lambda b, t: (0, t)),
            pl.BlockSpec((LANES, tm), lambda b, t: (0, t)),
            _resident(lng.shape),
            _resident(lnb.shape),
            _resident(sgu_w.shape),
            _resident(bs_full.shape),
        ],
        out_specs=[pl.BlockSpec((1, A_WIDTH, tm), tok_t), pl.BlockSpec((1, tm, A_WIDTH), tok),
                   pl.BlockSpec((1, A_WIDTH, tm), tok_t), pl.BlockSpec((1, tm, B_WIDTH), tok)],
        out_shape=[out_t, out, out_t, out],
        scratch_shapes=[pltpu.VMEM((tm, B_WIDTH), BF16)] * 2,
        compiler_params=_params(("arbitrary", "arbitrary")),
        name="proj_ab",
    )(x, w_kuv, w_qv_t, cos, sin, cos_t, sin_t, lng, lnb, sgu_w, bs_full)


def _moba_gate_kernel(qt_ref, k_ref, avg_ref, bias_ref):
    blk = MOBA_BLOCK
    gate_rows = avg_ref.shape[0]
    s_len = qt_ref.shape[2]
    width = min(s_len, MOBA_GATE_WIDTH)
    kmean = _dot(avg_ref[...], k_ref[0])
    lane_g = lax.broadcasted_iota(jnp.int32, kmean.shape, 1)
    blk_row = lax.broadcasted_iota(jnp.int32, (gate_rows, width), 0)
    feat_row = lax.broadcasted_iota(jnp.int32, (LANES, width), 0)
    qry_lane = lax.broadcasted_iota(jnp.int32, (1, width), 1)
    for c in range(s_len // width):
        q_pair = qt_ref[0, :, c * width:(c + 1) * width]
        q_blk = (qry_lane + c * width) // blk
        for h in range(2):
            head_rows = (feat_row < A_DH) if h == 0 else (feat_row >= A_DH)
            head_lanes = (lane_g < A_DH) if h == 0 else (lane_g >= A_DH)
            qh = jnp.where(head_rows, q_pair, jnp.zeros_like(q_pair))
            gate = _dot(jnp.where(head_lanes, kmean, 0.0).astype(BF16), qh)
            gate = jnp.where(blk_row < q_blk, gate, NEG)
            chosen = jnp.zeros((gate_rows, width), jnp.bool_)
            for _ in range(MOBA_TOPK):
                best = jnp.max(gate, axis=0, keepdims=True)
                idx = jnp.min(jnp.where(gate == best, blk_row, LANES), axis=0, keepdims=True)
                pick = blk_row == idx
                chosen = chosen | (pick & (idx < q_blk))
                gate = jnp.where(pick, -jnp.inf, gate)
            bias_ref[0, h, :, c * width:(c + 1) * width] = jnp.where(chosen, 0.0, NEG).astype(BF16)


def _moba_gate(q_t, k):
    bn, s, _ = k.shape
    blk = MOBA_BLOCK
    nb = s // blk
    assert nb <= LANES
    gate_rows = -(-nb // 16) * 16
    block_avg = ((jnp.arange(gate_rows)[:, None] == jnp.arange(s)[None, :] // blk) / blk).astype(BF16)
    return pl.pallas_call(
        _moba_gate_kernel,
        grid=(bn, A_WIDTH // LANES),
        in_specs=[
            pl.BlockSpec((1, LANES, s), lambda b, p: (b, p, 0)),
            pl.BlockSpec((1, s, LANES), lambda b, p: (b, 0, p)),
            _resident((gate_rows, s)),
        ],
        out_specs=pl.BlockSpec((1, 2, gate_rows, s), lambda b, p: (b, p, 0, 0)),
        out_shape=jax.ShapeDtypeStruct((bn, A_HEADS, gate_rows, s), BF16),
        compiler_params=_params(("arbitrary", "arbitrary")),
        name="moba_gate",
    )(q_t, k, block_avg)


def _moba_kernel(qt_ref, k_ref, vt_ref, bias_ref, e_ref, o_ref, *ring):
    i = pl.program_id(2)
    blk = MOBA_BLOCK
    nb = k_ref.shape[1] // blk
    pairs = qt_ref.shape[1] // LANES
    depth = len(ring) // pairs
    ahead = depth - 2
    gate_rows = bias_ref.shape[2]
    feat_row = lax.broadcasted_iota(jnp.int32, (LANES, blk), 0)
    key_idx = lax.broadcasted_iota(jnp.int32, (blk, blk), 0)
    qry_idx = lax.broadcasted_iota(jnp.int32, (blk, blk), 1)
    causal = key_idx <= qry_idx
    bias_pad = jnp.zeros((LANES - gate_rows, blk), BF16)
    ones_rows = jnp.ones((MOBA_SUM_ROWS, blk), BF16)
    own = pl.multiple_of(i * blk, blk)

    def key_start(j):
        return pl.multiple_of(jnp.minimum(j, nb - 1) * blk, blk)

    def values(head, start):
        return jnp.concatenate([vt_ref[0, head * A_DH:(head + 1) * A_DH, pl.ds(start, blk)], ones_rows], axis=0)

    def absorb(pair, s_both, start, carry, mask=None):
        out = []
        for h in range(2):
            s = s_both[:, h * blk:(h + 1) * blk]
            if mask is not None:
                s = jnp.where(mask, s, NEG)
            s = s.astype(BF16)
            m_new = jnp.max(s, axis=0, keepdims=True).astype(F32)
            pv = lambda p: _dot(values(2 * pair + h, start), p)
            if carry is None:
                out += [m_new, pv(jnp.exp2(s - m_new.astype(BF16)))]
            else:
                m_old, a_old = carry[2 * h], carry[2 * h + 1]
                m_new = jnp.maximum(m_old, m_new)
                out += [m_new, jnp.exp2(m_old - m_new) * a_old + pv(jnp.exp2(s - m_new.astype(BF16)))]
        return out

    rhs, carry = [], []
    for pair in range(pairs):
        rows = slice(pair * LANES, (pair + 1) * LANES)
        q_pair = qt_ref[0, rows, :]
        qh = [jnp.where(feat_row < A_DH, q_pair, jnp.zeros_like(q_pair)),
              jnp.where(feat_row >= A_DH, q_pair, jnp.zeros_like(q_pair))]
        rhs.append(jnp.concatenate([jnp.concatenate([qh[h], bias_ref[0, 2 * pair + h], bias_pad], axis=0)
                                    for h in range(2)], axis=1))
        s_own = _dot(k_ref[0, pl.ds(own, blk), rows], jnp.concatenate(qh, axis=1))
        carry += absorb(pair, s_own, own, None, causal)

    def scores(pair, j):
        start = key_start(j)
        lhs = jnp.concatenate([k_ref[0, pl.ds(start, blk), pair * LANES:(pair + 1) * LANES],
                               e_ref[pl.ds(start, blk), :]], axis=1)
        return _dot(lhs, rhs[pair])

    for pair in range(pairs):
        for r in range(ahead):
            ring[pair * depth + r][...] = scores(pair, r)

    def body(it, carry):
        carry = list(carry)
        for r in range(depth):
            j = depth * it + r
            for pair in range(pairs):
                ring[pair * depth + (r + ahead) % depth][...] = scores(pair, j + ahead)
                carry[4 * pair:4 * pair + 4] = absorb(pair, ring[pair * depth + r][...], key_start(j),
                                                      carry[4 * pair:4 * pair + 4])
        return tuple(carry)

    carry = lax.fori_loop(0, (i + depth - 1) // depth, body, tuple(carry))
    out_t = jnp.concatenate([acc[:A_DH] / acc[A_DH:A_DH + 1] for acc in carry[1::2]], axis=0)
    o_ref[0] = out_t.T.astype(BF16)


def _moba(q_t, k, v_t, bias):
    bn, s, _ = k.shape
    blk = MOBA_BLOCK
    nb = s // blk
    gate_rows = bias.shape[2]
    width = MOBA_PAIRS * LANES
    block_onehot = (jnp.arange(s)[:, None] // blk == jnp.arange(LANES)[None, :]).astype(BF16)
    return pl.pallas_call(
        _moba_kernel,
        grid=(bn, A_WIDTH // width, nb),
        in_specs=[
            pl.BlockSpec((1, width, blk), lambda b, p, i: (b, p, i)),
            pl.BlockSpec((1, s, width), lambda b, p, i: (b, 0, p)),
            pl.BlockSpec((1, width, s), lambda b, p, i: (b, p, 0)),
            pl.BlockSpec((1, 2 * MOBA_PAIRS, gate_rows, blk), lambda b, p, i: (b, p, 0, i)),
            _resident((s, LANES)),
        ],
        out_specs=pl.BlockSpec((1, blk, width), lambda b, p, i: (b, i, p)),
        out_shape=jax.ShapeDtypeStruct((bn, s, A_WIDTH), BF16),
        scratch_shapes=[pltpu.VMEM((blk, 2 * blk), F32)] * (MOBA_PAIRS * MOBA_RING),
        compiler_params=_params(("arbitrary", "arbitrary", "arbitrary")),
        name="moba",
    )(q_t, k, v_t, bias, block_onehot)


def _out_ffn_kernel(n_parts, *refs):
    x_ref = refs[0]
    part_refs = refs[1:1 + n_parts]
    w_ref, g1_ref, b1_ref, wgu_ref, wd_ref, g2_ref, b2_ref, o_ref, acc_ref = refs[1 + n_parts:]
    y = None
    lo = 0
    for p_ref in part_refs:
        width = p_ref.shape[-1]
        term = _dot(p_ref[...], w_ref[lo:lo + width, :])
        y = term if y is None else y + term
        lo += width
    mid = _layer_norm(ALPHA * x_ref[...] + y, g1_ref[...], b1_ref[...])
    o_ref[...] = _ffn_ln_value(mid, wgu_ref, wd_ref, g2_ref, b2_ref, acc_ref)


def _out_ffn(x2, parts, w_out, w_gu, w_down_half, ln_g, ln_b, l):
    n = x2.shape[0]
    tm = TOKEN_TILE
    tok = lambda i: (i, 0)
    return pl.pallas_call(
        functools.partial(_out_ffn_kernel, len(parts)),
        grid=(n // tm,),
        in_specs=[pl.BlockSpec((tm, D_MODEL), tok)]
        + [pl.BlockSpec((tm, p.shape[-1]), tok) for p in parts]
        + [_resident(w_out.shape), _picked(ln_g, (l, 1)), _picked(ln_b, (l, 1)),
           _picked(w_gu, (l, 1)), _picked(w_down_half, (l, 1)), _picked(ln_g, (l, 2)), _picked(ln_b, (l, 2))],
        out_specs=pl.BlockSpec((tm, D_MODEL), tok),
        out_shape=jax.ShapeDtypeStruct((n, D_MODEL), F32),
        scratch_shapes=[pltpu.VMEM((tm, D_MODEL), F32)],
        compiler_params=_params(("arbitrary",)),
        name="out_ffn",
    )(x2, *parts, w_out, ln_g, ln_b, w_gu, w_down_half, ln_g, ln_b)


def _proj_c_kernel(x_ref, wqk_ref, wvt_ref, wot_ref, wif_ref, wift_ref, cw_ref, cb_ref, gb_ref, gbt_ref,
                   q_ref, k_ref, vt_ref, ogt_ref, gates_ref, gatest_ref, *pre_refs):
    t = pl.program_id(1)
    tm = x_ref.shape[1]
    xb = x_ref[0].astype(BF16)
    width = pre_refs[0].shape[1]

    @pl.when(t == 0)
    def _():
        for pre_ref in pre_refs:
            pre_ref[:SUBLANES, :] = jnp.zeros((SUBLANES, width), F32)

    gates_ref[0] = _dot(xb, wif_ref[...]) + gb_ref[...]
    gatest_ref[0] = _dot_nt(wift_ref[...], xb) + gbt_ref[...]
    for c, pre_ref in enumerate(pre_refs):
        pre_ref[SUBLANES:, :] = _dot(xb, wqk_ref[:, c * width:(c + 1) * width])
    for c, pre_ref in enumerate(pre_refs):
        cols = slice(c * width, (c + 1) * width)
        vt_ref[0, cols, :] = _dot_nt(wvt_ref[cols, :], xb).astype(BF16)
        conv = cb_ref[:, cols] + cw_ref[CONV_W - 1:CONV_W, cols] * pre_ref[SUBLANES:, :]
        for d in range(1, CONV_W):
            conv = conv + cw_ref[CONV_W - 1 - d:CONV_W - d, cols] * pre_ref[SUBLANES - d:SUBLANES - d + tm, :]
        pre_ref[:SUBLANES, :] = pre_ref[tm:, :]
        act = jax.nn.silu(conv)
        if c * width < C_QK_WIDTH:
            q_ref[0, :, cols] = act.astype(BF16)
        else:
            k_cols = slice(c * width - C_QK_WIDTH, (c + 1) * width - C_QK_WIDTH)
            k_ref[0, :, k_cols] = (act * (C_DQK ** -0.5)).astype(BF16)
        ogt_ref[0, cols, :] = jax.nn.sigmoid(_dot_nt(wot_ref[cols, :], xb)).astype(BF16)


def _proj_c(x, wqk, wv_t, wo_t, wif, wift, conv_w, conv_b, gb, gbt):
    bn, s, _ = x.shape
    tm = min(s, PROJ_TILE)
    tok = lambda b, t: (b, t, 0)
    return pl.pallas_call(
        _proj_c_kernel,
        grid=(bn, s // tm),
        in_specs=[pl.BlockSpec((1, tm, D_MODEL), tok)]
        + [_resident(a.shape) for a in (wqk, wv_t, wo_t, wif, wift, conv_w, conv_b, gb, gbt)],
        out_specs=[
            pl.BlockSpec((1, tm, C_QK_WIDTH), tok),
            pl.BlockSpec((1, tm, C_QK_WIDTH), tok),
            pl.BlockSpec((1, C_WIDTH, tm), lambda b, t: (b, 0, t)),
            pl.BlockSpec((1, C_WIDTH, tm), lambda b, t: (b, 0, t)),
            pl.BlockSpec((1, tm, LANES), tok),
            pl.BlockSpec((1, SUBLANES, tm), lambda b, t: (b, 0, t)),
        ],
        out_shape=[
            jax.ShapeDtypeStruct((bn, s, C_QK_WIDTH), BF16),
            jax.ShapeDtypeStruct((bn, s, C_QK_WIDTH), BF16),
            jax.ShapeDtypeStruct((bn, C_WIDTH, s), BF16),
            jax.ShapeDtypeStruct((bn, C_WIDTH, s), BF16),
            jax.ShapeDtypeStruct((bn, s, LANES), F32),
            jax.ShapeDtypeStruct((bn, SUBLANES, s), F32),
        ],
        scratch_shapes=[pltpu.VMEM((SUBLANES + tm, PROJ_CHUNK), F32)] * (2 * C_QK_WIDTH // PROJ_CHUNK),
        compiler_params=_params(("arbitrary", "arbitrary")),
        name="proj_c",
    )(x, wqk, wv_t, wo_t, wif, wift, conv_w, conv_b, gb, gbt)


def _split3(a):
    hi = a.astype(BF16)
    r1 = a - hi.astype(F32)
    mid = r1.astype(BF16)
    lo = (r1 - mid.astype(F32)).astype(BF16)
    return hi, mid, lo


def _mlstm_kernel(q_ref, k_ref, vt_ref, ogt_ref, gates_ref, gatest_ref, hg_ref, o_ref,
                  ct_ref, n_ref, m_ref):
    c_idx = pl.program_id(1)
    L = q_ref.shape[1]

    @pl.when(c_idx == 0)
    def _():
        ct_ref[...] = jnp.zeros_like(ct_ref)
        n_ref[...] = jnp.zeros_like(n_ref)
        m_ref[...] = jnp.zeros_like(m_ref)

    src_idx = lax.broadcasted_iota(jnp.int32, (L, L), 0)
    tgt_idx = lax.broadcasted_iota(jnp.int32, (L, L), 1)
    causal = src_idx <= tgt_idx
    tri = (tgt_idx <= src_idx).astype(BF16)
    tri_t = causal.astype(BF16)

    gates = gates_ref[0]
    gates_t = gatest_ref[0]
    b_cols = sum(_dot(tri, part) for part in _split3(jax.nn.log_sigmoid(gates)))
    b_rows = sum(_dot(part, tri_t) for part in _split3(jax.nn.log_sigmoid(gates_t)))

    for h in range(C_HEADS):
        qk_cols = slice(h * C_DQK, (h + 1) * C_DQK)
        v_rows = slice(h * C_DV, (h + 1) * C_DV)
        c_col = gates[:, h:h + 1] - b_cols[:, C_HEADS + h:C_HEADS + h + 1]
        i_row = gates_t[h:h + 1, :]
        b_row = b_rows[C_HEADS + h:C_HEADS + h + 1, :]
        b_end = b_row[:, L - 1:L]
        m_in = m_ref[h][:, 0:1]
        ct_in = ct_ref[h]
        n_in = n_ref[h]

        masked_c = jnp.where(causal, c_col, -jnp.inf)
        m_row = jnp.maximum(m_in, jnp.max(masked_c, axis=0, keepdims=True))
        p_t = jnp.exp(masked_c - m_row)
        inter = jnp.exp(m_in - m_row)

        q_t = q_ref[0, :, qk_cols].astype(F32).T.astype(BF16)
        kh = k_ref[0, :, qk_cols]
        vt_h = vt_ref[0, v_rows, :]
        sqk_t = _dot(kh, q_t) * p_t
        num_t = _dot(vt_h, sqk_t.astype(BF16)) + inter * _dot(ct_in.astype(BF16), q_t)
        n_q = _dot(jnp.broadcast_to(n_in, (SUBLANES, C_DQK)).astype(BF16), q_t)[0:1]
        den = jnp.sum(sqk_t, axis=0, keepdims=True) + inter * n_q
        hid_t = num_t * (1.0 / jnp.maximum(jnp.abs(den), jnp.exp(-(b_row + m_row))))
        mu = jnp.mean(hid_t, axis=0, keepdims=True)
        zc = hid_t - mu
        var = jnp.mean(zc * zc, axis=0, keepdims=True)
        out_t = ogt_ref[0, v_rows, :].astype(F32) * (zc * lax.rsqrt(var + LN_EPS) * hg_ref[v_rows, :])
        o_ref[0, :, v_rows] = out_t.T.astype(BF16)

        a_row = b_end - b_row + i_row
        a_max = jnp.max(a_row, axis=1, keepdims=True)
        w_end = jnp.exp(a_row - a_max)
        m_new = jnp.maximum(b_end + m_in, a_max)
        decay = jnp.exp(b_end + m_in - m_new)
        inject = jnp.exp(a_max - m_new)
        vw = (vt_h.astype(F32) * w_end).astype(BF16)
        ct_ref[h] = decay * ct_in + inject * _dot(vw, kh)
        k_sum = _dot(jnp.broadcast_to(w_end, (SUBLANES, L)).astype(BF16), kh)[0:1]
        n_ref[h] = decay * n_in + inject * k_sum
        m_ref[h] = jnp.broadcast_to(m_new, (1, LANES))


def _mlstm(q, k, v_t, og_t, gates, gates_t, head_g):
    bn, s, _ = q.shape
    L = MLSTM_L
    tok = lambda b, c: (b, c, 0)
    feat = lambda b, c: (b, 0, c)
    hg_cols = jnp.broadcast_to(head_g.reshape(C_WIDTH, 1), (C_WIDTH, L))
    return pl.pallas_call(
        _mlstm_kernel,
        grid=(bn, s // L),
        in_specs=[
            pl.BlockSpec((1, L, C_QK_WIDTH), tok),
            pl.BlockSpec((1, L, C_QK_WIDTH), tok),
            pl.BlockSpec((1, C_WIDTH, L), feat),
            pl.BlockSpec((1, C_WIDTH, L), feat),
            pl.BlockSpec((1, L, LANES), tok),
            pl.BlockSpec((1, SUBLANES, L), feat),
            _resident((C_WIDTH, L)),
        ],
        out_specs=pl.BlockSpec((1, L, C_WIDTH), tok),
        out_shape=jax.ShapeDtypeStruct((bn, s, C_WIDTH), BF16),
        scratch_shapes=[
            pltpu.VMEM((C_HEADS, C_DV, C_DQK), F32),
            pltpu.VMEM((C_HEADS, 1, C_DQK), F32),
            pltpu.VMEM((C_HEADS, 1, LANES), F32),
        ],
        compiler_params=_params(("arbitrary", "arbitrary")),
        name="mlstm",
    )(q, k, v_t, og_t, gates, gates_t, hg_cols)


def _rope_tables(s):
    half = A_DH // 2
    inv = ROPE_THETA ** (-jnp.arange(half, dtype=F32) / half)
    ang = jnp.arange(s).astype(F32)[:, None] * inv[None, :]
    cos, sin = jnp.cos(ang), jnp.sin(ang)
    cos = jnp.tile(cos, (1, LANES // half))
    sin = jnp.tile(jnp.concatenate([-sin, sin], axis=1), (1, LANES // A_DH))
    return cos, sin, cos.T, sin.T


def _mixer_ab(x, w_in, sgu_ln_g, sgu_ln_b, sgu_w, sgu_b):
    bn, s, d = x.shape
    cos, sin, cos_t, sin_t = _rope_tables(s)
    w_kuv = jnp.concatenate([w_in[:, A_WIDTH:2 * A_WIDTH], w_in[:, 3 * A_WIDTH:]], axis=1).astype(BF16)
    w_qv_t = jnp.concatenate([w_in[:, :A_WIDTH], w_in[:, 2 * A_WIDTH:3 * A_WIDTH]], axis=1).T.astype(BF16)
    bs_full = jnp.repeat(sgu_b.T, B_DG, axis=1)
    q_t, k, v_t, bg = _proj_ab(x, w_kuv, w_qv_t, cos, sin, cos_t, sin_t,
                               sgu_ln_g.reshape(1, B_WIDTH), sgu_ln_b.reshape(1, B_WIDTH), sgu_w, bs_full)
    a = _moba(q_t, k, v_t, _moba_gate(q_t, k))
    return [a.reshape(bn * s, A_WIDTH), bg.reshape(bn * s, B_WIDTH)]


def _mixer_c(x, w_in, conv_w, conv_b, b_i, b_f, head_g):
    bn, s, d = x.shape
    qk_w = 2 * C_QK_WIDTH
    wqk = w_in[:, :qk_w].astype(BF16)
    wv_t = w_in[:, qk_w:qk_w + C_WIDTH].T.astype(BF16)
    wo_t = w_in[:, qk_w + C_WIDTH:qk_w + 2 * C_WIDTH].T.astype(BF16)
    w_if = w_in[:, qk_w + 2 * C_WIDTH:]
    wif = jnp.pad(w_if, ((0, 0), (0, LANES - 2 * C_HEADS))).astype(BF16)
    wift = w_if.T.astype(BF16)
    gate_bias = jnp.concatenate([b_i, b_f]).astype(F32)
    gb = jnp.pad(gate_bias, (0, LANES - 2 * C_HEADS)).reshape(1, LANES)
    gbt = gate_bias.reshape(2 * C_HEADS, 1)
    q, k, v_t, og_t, gates, gates_t = _proj_c(x, wqk, wv_t, wo_t, wif, wift, conv_w,
                                              conv_b.reshape(1, qk_w), gb, gbt)
    hg = _mlstm(q, k, v_t, og_t, gates, gates_t, head_g)
    return [hg.reshape(bn * s, C_WIDTH)]


def kernel(x, ln_g, ln_b, ffn_w_gu, ffn_w_down, ab_w_in, sgu_ln_g, sgu_ln_b, sgu_w, sgu_b, ab_w_out,
           c_w_in, c_conv_w, c_conv_b, c_b_i, c_b_f, c_head_g, c_w_out):
    bn, s, d = x.shape
    n = bn * s

    w_gu = ffn_w_gu.astype(BF16)
    w_down_half = (0.5 * ffn_w_down).astype(BF16)
    ln_g = ln_g.reshape(DEPTH, 3, 1, d)
    ln_b = ln_b.reshape(DEPTH, 3, 1, d)
    x = x.reshape(n, d)
    for l in range(DEPTH):
        x = _ffn_ln(x, w_gu, w_down_half, ln_g, ln_b, l, 0, 0)
        j = l // 2
        if l % 2 == 0:
            parts = _mixer_ab(x.reshape(bn, s, d), ab_w_in[j], sgu_ln_g[j], sgu_ln_b[j], sgu_w[j], sgu_b[j])
            w_out = ab_w_out[j]
        else:
            parts = _mixer_c(x.reshape(bn, s, d), c_w_in[j], c_conv_w[j], c_conv_b[j], c_b_i[j], c_b_f[j],
                             c_head_g[j])
            w_out = c_w_out[j]
        x = _out_ffn(x, parts, w_out.astype(BF16), w_gu, w_down_half, ln_g, ln_b, l)
    return x.reshape(bn, s, d)
```

```python
import functools

import jax
import jax.numpy as jnp
from jax import lax
from jax.experimental import pallas as pl
from jax.experimental.pallas import tpu as pltpu

D_MODEL = 1024
DEPTH = 2
D_FF = 2816
ALPHA = (2 * DEPTH) ** 0.25
LN_EPS = 1e-5
NEG = -1e30

A_HEADS = 8
A_DH = 64
A_WIDTH = A_HEADS * A_DH
MOBA_BLOCK = 256
MOBA_TOPK = 3
ROPE_THETA = 10000.0

B_GROUPS = 8
B_DG = 64
B_WIDTH = B_GROUPS * B_DG
SGU_CHUNK = 128

C_HEADS = 4
C_DQK = 128
C_DV = 256
C_QK_WIDTH = C_HEADS * C_DQK
C_WIDTH = C_HEADS * C_DV
CONV_W = 4

LANES = 128
SUBLANES = 8
VMEM_LIMIT = 56 * 1024 * 1024

TOKEN_TILE = 512
PROJ_TILE = 1024
FF_CHUNK = 256
PROJ_CHUNK = 256
MLSTM_L = 256
MOBA_RING = 3
MOBA_PAIRS = 4
MOBA_SUM_ROWS = 16
MOBA_GATE_WIDTH = 1024
LOG2E = 1.4426950408889634

BF16 = jnp.bfloat16
F32 = jnp.float32


def _params(sem):
    return pltpu.CompilerParams(dimension_semantics=sem, vmem_limit_bytes=VMEM_LIMIT)


def _dot(a, b):
    return jnp.dot(a, b, preferred_element_type=F32)


def _dot_nt(a, b):
    return lax.dot_general(a, b, (((1,), (1,)), ((), ())), preferred_element_type=F32)


def _layer_norm(z, g, b):
    mu = jnp.mean(z, axis=-1, keepdims=True)
    zc = z - mu
    var = jnp.mean(zc * zc, axis=-1, keepdims=True)
    y = zc * lax.rsqrt(var + LN_EPS) * g
    return y if b is None else y + b


def _resident(shape):
    return pl.BlockSpec(shape, lambda *_: (0,) * len(shape), pipeline_mode=pl.Buffered(1))


def _picked(arr, lead):
    rest = arr.shape[len(lead):]
    return pl.BlockSpec((None,) * len(lead) + rest, lambda *_: tuple(lead) + (0,) * len(rest),
                        pipeline_mode=pl.Buffered(1))


def _ffn_ln_value(x, wgu_ref, wd_ref, g_ref, b_ref, acc_ref):
    xb = x.astype(BF16)
    for c in range(D_FF // FF_CHUNK):
        lo = c * FF_CHUNK
        gate = _dot(xb, wgu_ref[:, lo:lo + FF_CHUNK])
        up = _dot(xb, wgu_ref[:, D_FF + lo:D_FF + lo + FF_CHUNK])
        h = (jax.nn.silu(gate) * up).astype(BF16)
        part = _dot(h, wd_ref[lo:lo + FF_CHUNK, :])
        if c == 0:
            acc_ref[...] = part
        else:
            acc_ref[...] += part
    z = ALPHA * x + acc_ref[...]
    return _layer_norm(z, g_ref[...], b_ref[...])


def _ffn_ln_kernel(x_ref, wgu_ref, wd_ref, g_ref, b_ref, o_ref, acc_ref):
    o_ref[...] = _ffn_ln_value(x_ref[...], wgu_ref, wd_ref, g_ref, b_ref, acc_ref)


def _ffn_ln(x2, w_gu, w_down_half, ln_g, ln_b, l, j, ln_idx):
    n = x2.shape[0]
    tm = TOKEN_TILE
    return pl.pallas_call(
        _ffn_ln_kernel,
        grid=(n // tm,),
        in_specs=[pl.BlockSpec((tm, D_MODEL), lambda i: (i, 0)),
                  _picked(w_gu, (l, j)), _picked(w_down_half, (l, j)),
                  _picked(ln_g, (l, ln_idx)), _picked(ln_b, (l, ln_idx))],
        out_specs=pl.BlockSpec((tm, D_MODEL), lambda i: (i, 0)),
        out_shape=jax.ShapeDtypeStruct((n, D_MODEL), F32),
        scratch_shapes=[pltpu.VMEM((tm, D_MODEL), F32)],
        compiler_params=_params(("arbitrary",)),
        name="ffn_ln",
    )(x2, w_gu, w_down_half, ln_g, ln_b)


def _swap_halves_lanes(t, first_half):
    fwd = pltpu.roll(t, A_DH // 2, axis=1)
    bwd = pltpu.roll(t, LANES - A_DH // 2, axis=1)
    return jnp.where(first_half, bwd, fwd)


def _swap_halves_rows(t):
    half = A_DH // 2
    return jnp.concatenate([t[half:2 * half], t[:half], t[3 * half:], t[2 * half:3 * half]], axis=0)


def _sgu_tile(u_ref, vn_ref, w_ref, bs_ref, o_ref):
    tm = u_ref.shape[0]
    t_row = lax.broadcasted_iota(jnp.int32, (SGU_CHUNK, SGU_CHUNK), 0)
    t_col = lax.broadcasted_iota(jnp.int32, (SGU_CHUNK, SGU_CHUNK), 1)
    causal = t_col <= t_row
    lane = lax.broadcasted_iota(jnp.int32, (SGU_CHUNK, LANES), 1)
    lo_mask = lane < B_DG
    for pair in range(B_WIDTH // LANES):
        w_lo = jnp.where(causal, w_ref[2 * pair], 0.0).astype(BF16)
        w_hi = jnp.where(causal, w_ref[2 * pair + 1], 0.0).astype(BF16)
        cols = slice(pair * LANES, (pair + 1) * LANES)
        bias = bs_ref[:, cols]
        for c in range(tm // SGU_CHUNK):
            rows = slice(c * SGU_CHUNK, (c + 1) * SGU_CHUNK)
            vp = vn_ref[rows, cols]
            zero = jnp.zeros_like(vp)
            mixed = _dot(w_lo, jnp.where(lo_mask, vp, zero)) + _dot(w_hi, jnp.where(lo_mask, zero, vp))
            o_ref[0, rows, cols] = (u_ref[rows, cols].astype(F32) * (mixed + bias)).astype(BF16)


def _proj_ab_kernel(x_ref, w_ref, wqvt_ref, cos_ref, sin_ref, cost_ref, sint_ref, lng_ref, lnb_ref,
                    sw_ref, sb_ref, qt_ref, k_ref, vt_ref, bg_ref, u_ref, vn_ref):
    xb = x_ref[0].astype(BF16)
    cos = cos_ref[...]
    sin = sin_ref[...]
    cos_t = cost_ref[...]
    sin_t = sint_ref[...]
    lane = lax.broadcasted_iota(jnp.int32, cos.shape, 1)
    first_half = (lane % A_DH) < (A_DH // 2)
    q_t = _dot_nt(wqvt_ref[:A_WIDTH, :], xb)
    for grp in range(A_WIDTH // LANES):
        lo = grp * LANES
        qg = q_t[lo:lo + LANES]
        qg = qg * cos_t + _swap_halves_rows(qg) * sin_t
        qt_ref[0, lo:lo + LANES, :] = (qg * (A_DH ** -0.5 * LOG2E)).astype(BF16)
        k = _dot(xb, w_ref[:, lo:lo + LANES])
        k = k * cos + _swap_halves_lanes(k, first_half) * sin
        k_ref[0, :, lo:lo + LANES] = k.astype(BF16)
    vt_ref[0] = _dot_nt(wqvt_ref[A_WIDTH:, :], xb).astype(BF16)
    ub = _dot(xb, w_ref[:, A_WIDTH:A_WIDTH + B_WIDTH])
    u_ref[...] = jax.nn.gelu(ub).astype(BF16)
    vb = _dot(xb, w_ref[:, A_WIDTH + B_WIDTH:])
    vn_ref[...] = _layer_norm(jax.nn.gelu(vb), lng_ref[...], lnb_ref[...]).astype(BF16)
    _sgu_tile(u_ref, vn_ref, sw_ref, sb_ref, bg_ref)


def _proj_ab(x, w_kuv, w_qv_t, cos, sin, cos_t, sin_t, lng, lnb, sgu_w, bs_full):
    bn, s, _ = x.shape
    tm = min(s, PROJ_TILE)
    tok = lambda b, t: (b, t, 0)
    tok_t = lambda b, t: (b, 0, t)
    out = jax.ShapeDtypeStruct((bn, s, A_WIDTH), BF16)
    out_t = jax.ShapeDtypeStruct((bn, A_WIDTH, s), BF16)
    return pl.pallas_call(
        _proj_ab_kernel,
        grid=(bn, s // tm),
        in_specs=[
            pl.BlockSpec((1, tm, D_MODEL), tok),
            _resident(w_kuv.shape),
            _resident(w_qv_t.shape),
            pl.BlockSpec((tm, LANES), lambda b, t: (t, 0)),
            pl.BlockSpec((tm, LANES), lambda b, t: (t, 0)),
            pl.BlockSpec((LANES, tm), lambda b, t: (0, t)),
            pl.BlockSpec((LANES, tm), lambda b, t: (0, t)),
            _resident(lng.shape),
            _resident(lnb.shape),
            _resident(sgu_w.shape),
            _resident(bs_full.shape),
        ],
        out_specs=[pl.BlockSpec((1, A_WIDTH, tm), tok_t), pl.BlockSpec((1, tm, A_WIDTH), tok),
                   pl.BlockSpec((1, A_WIDTH, tm), tok_t), pl.BlockSpec((1, tm, B_WIDTH), tok)],
        out_shape=[out_t, out, out_t, out],
        scratch_shapes=[pltpu.VMEM((tm, B_WIDTH), BF16)] * 2,
        compiler_params=_params(("arbitrary", "arbitrary")),
        name="proj_ab",
    )(x, w_kuv, w_qv_t, cos, sin, cos_t, sin_t, lng, lnb, sgu_w, bs_full)


def _moba_gate_kernel(qt_ref, k_ref, avg_ref, bias_ref):
    blk = MOBA_BLOCK
    gate_rows = avg_ref.shape[0]
    s_len = qt_ref.shape[2]
    width = min(s_len, MOBA_GATE_WIDTH)
    kmean = _dot(avg_ref[...], k_ref[0])
    lane_g = lax.broadcasted_iota(jnp.int32, kmean.shape, 1)
    blk_row = lax.broadcasted_iota(jnp.int32, (gate_rows, width), 0)
    feat_row = lax.broadcasted_iota(jnp.int32, (LANES, width), 0)
    qry_lane = lax.broadcasted_iota(jnp.int32, (1, width), 1)
    for c in range(s_len // width):
        q_pair = qt_ref[0, :, c * width:(c + 1) * width]
        q_blk = (qry_lane + c * width) // blk
        for h in range(2):
            head_rows = (feat_row < A_DH) if h == 0 else (feat_row >= A_DH)
            head_lanes = (lane_g < A_DH) if h == 0 else (lane_g >= A_DH)
            qh = jnp.where(head_rows, q_pair, jnp.zeros_like(q_pair))
            gate = _dot(jnp.where(head_lanes, kmean, 0.0).astype(BF16), qh)
            gate = jnp.where(blk_row < q_blk, gate, NEG)
            chosen = jnp.zeros((gate_rows, width), jnp.bool_)
            for _ in range(MOBA_TOPK):
                best = jnp.max(gate, axis=0, keepdims=True)
                idx = jnp.min(jnp.where(gate == best, blk_row, LANES), axis=0, keepdims=True)
                pick = blk_row == idx
                chosen = chosen | (pick & (idx < q_blk))
                gate = jnp.where(pick, -jnp.inf, gate)
            bias_ref[0, h, :, c * width:(c + 1) * width] = jnp.where(chosen, 0.0, NEG).astype(BF16)


def _moba_gate(q_t, k):
    bn, s, _ = k.shape
    blk = MOBA_BLOCK
    nb = s // blk
    assert nb <= LANES
    gate_rows = -(-nb // 16) * 16
    block_avg = ((jnp.arange(gate_rows)[:, None] == jnp.arange(s)[None, :] // blk) / blk).astype(BF16)
    return pl.pallas_call(
        _moba_gate_kernel,
        grid=(bn, A_WIDTH // LANES),
        in_specs=[
            pl.BlockSpec((1, LANES, s), lambda b, p: (b, p, 0)),
            pl.BlockSpec((1, s, LANES), lambda b, p: (b, 0, p)),
            _resident((gate_rows, s)),
        ],
        out_specs=pl.BlockSpec((1, 2, gate_rows, s), lambda b, p: (b, p, 0, 0)),
        out_shape=jax.ShapeDtypeStruct((bn, A_HEADS, gate_rows, s), BF16),
        compiler_params=_params(("arbitrary", "arbitrary")),
        name="moba_gate",
    )(q_t, k, block_avg)


def _moba_kernel(qt_ref, k_ref, vt_ref, bias_ref, e_ref, o_ref, cmax_ref, *ring):
    i = pl.program_id(2)
    blk = MOBA_BLOCK
    nb = k_ref.shape[1] // blk
    pairs = qt_ref.shape[1] // LANES
    depth = len(ring) // pairs
    ahead = depth - 2
    gate_rows = bias_ref.shape[2]
    feat_row = lax.broadcasted_iota(jnp.int32, (LANES, blk), 0)
    key_idx = lax.broadcasted_iota(jnp.int32, (blk, blk), 0)
    qry_idx = lax.broadcasted_iota(jnp.int32, (blk, blk), 1)
    causal = key_idx <= qry_idx
    bias_pad = jnp.zeros((LANES - gate_rows, blk), BF16)
    ones_rows = jnp.ones((MOBA_SUM_ROWS, blk), BF16)
    own = pl.multiple_of(i * blk, blk)

    def key_start(j):
        return pl.multiple_of(jnp.minimum(j, nb - 1) * blk, blk)

    def values(head, start):
        return jnp.concatenate([vt_ref[0, head * A_DH:(head + 1) * A_DH, pl.ds(start, blk)], ones_rows], axis=0)

    def absorb(pair, s_both, start, carry, mask=None, col_max=None):
        out = []
        for h in range(2):
            s = s_both[:, h * blk:(h + 1) * blk]
            if mask is not None:
                s = jnp.where(mask, s, NEG)
            s = s.astype(BF16)
            if col_max is None:
                m_new = jnp.max(s, axis=0, keepdims=True).astype(F32)
            else:
                m_new = col_max[:, h * blk:(h + 1) * blk]
            pv = lambda p: _dot(values(2 * pair + h, start), p)
            if carry is None:
                out += [m_new, pv(jnp.exp2(s - m_new.astype(BF16)))]
            else:
                m_old, a_old = carry[2 * h], carry[2 * h + 1]
                m_new = jnp.maximum(m_old, m_new)
                out += [m_new, jnp.exp2(m_old - m_new) * a_old + pv(jnp.exp2(s - m_new.astype(BF16)))]
        return out

    rhs, carry = [], []
    for pair in range(pairs):
        rows = slice(pair * LANES, (pair + 1) * LANES)
        q_pair = qt_ref[0, rows, :]
        qh = [jnp.where(feat_row < A_DH, q_pair, jnp.zeros_like(q_pair)),
              jnp.where(feat_row >= A_DH, q_pair, jnp.zeros_like(q_pair))]
        rhs.append(jnp.concatenate([jnp.concatenate([qh[h], bias_ref[0, 2 * pair + h], bias_pad], axis=0)
                                    for h in range(2)], axis=1))
        s_own = _dot(k_ref[0, pl.ds(own, blk), rows], jnp.concatenate(qh, axis=1))
        carry += absorb(pair, s_own, own, None, causal)

    def scores(pair, j):
        start = key_start(j)
        lhs = jnp.concatenate([k_ref[0, pl.ds(start, blk), pair * LANES:(pair + 1) * LANES],
                               e_ref[pl.ds(start, blk), :]], axis=1)
        return _dot(lhs, rhs[pair])

    def issue(pair, slot, j):
        s_new = scores(pair, j).astype(BF16)
        ring[pair * depth + slot][...] = s_new
        cmax_ref[pair * depth + slot] = jnp.max(s_new, axis=0, keepdims=True).astype(F32)

    for pair in range(pairs):
        for r in range(ahead):
            issue(pair, r, r)

    def body(it, carry):
        carry = list(carry)
        for r in range(depth):
            j = depth * it + r
            for pair in range(pairs):
                issue(pair, (r + ahead) % depth, j + ahead)
                carry[4 * pair:4 * pair + 4] = absorb(pair, ring[pair * depth + r][...], key_start(j),
                                                      carry[4 * pair:4 * pair + 4],
                                                      col_max=cmax_ref[pair * depth + r])
        return tuple(carry)

    carry = lax.fori_loop(0, (i + depth - 1) // depth, body, tuple(carry))
    out_t = jnp.concatenate([acc[:A_DH] / acc[A_DH:A_DH + 1] for acc in carry[1::2]], axis=0)
    o_ref[0] = out_t.T.astype(BF16)


def _moba(q_t, k, v_t, bias):
    bn, s, _ = k.shape
    blk = MOBA_BLOCK
    nb = s // blk
    gate_rows = bias.shape[2]
    width = MOBA_PAIRS * LANES
    block_onehot = (jnp.arange(s)[:, None] // blk == jnp.arange(LANES)[None, :]).astype(BF16)
    return pl.pallas_call(
        _moba_kernel,
        grid=(bn, A_WIDTH // width, nb),
        in_specs=[
            pl.BlockSpec((1, width, blk), lambda b, p, i: (b, p, i)),
            pl.BlockSpec((1, s, width), lambda b, p, i: (b, 0, p)),
            pl.BlockSpec((1, width, s), lambda b, p, i: (b, p, 0)),
            pl.BlockSpec((1, 2 * MOBA_PAIRS, gate_rows, blk), lambda b, p, i: (b, p, 0, i)),
            _resident((s, LANES)),
        ],
        out_specs=pl.BlockSpec((1, blk, width), lambda b, p, i: (b, i, p)),
        out_shape=jax.ShapeDtypeStruct((bn, s, A_WIDTH), BF16),
        scratch_shapes=[pltpu.VMEM((MOBA_PAIRS * MOBA_RING, 1, 2 * blk), F32)]
        + [pltpu.VMEM((blk, 2 * blk), BF16)] * (MOBA_PAIRS * MOBA_RING),
        compiler_params=_params(("arbitrary", "arbitrary", "arbitrary")),
        name="moba",
    )(q_t, k, v_t, bias, block_onehot)


def _out_ffn_kernel(n_parts, *refs):
    x_ref = refs[0]
    part_refs = refs[1:1 + n_parts]
    w_ref, g1_ref, b1_ref, wgu_ref, wd_ref, g2_ref, b2_ref, o_ref, acc_ref = refs[1 + n_parts:]
    y = None
    lo = 0
    for p_ref in part_refs:
        width = p_ref.shape[-1]
        term = _dot(p_ref[...], w_ref[lo:lo + width, :])
        y = term if y is None else y + term
        lo += width
    mid = _layer_norm(ALPHA * x_ref[...] + y, g1_ref[...], b1_ref[...])
    o_ref[...] = _ffn_ln_value(mid, wgu_ref, wd_ref, g2_ref, b2_ref, acc_ref)


def _out_ffn(x2, parts, w_out, w_gu, w_down_half, ln_g, ln_b, l):
    n = x2.shape[0]
    tm = TOKEN_TILE
    tok = lambda i: (i, 0)
    return pl.pallas_call(
        functools.partial(_out_ffn_kernel, len(parts)),
        grid=(n // tm,),
        in_specs=[pl.BlockSpec((tm, D_MODEL), tok)]
        + [pl.BlockSpec((tm, p.shape[-1]), tok) for p in parts]
        + [_resident(w_out.shape), _picked(ln_g, (l, 1)), _picked(ln_b, (l, 1)),
           _picked(w_gu, (l, 1)), _picked(w_down_half, (l, 1)), _picked(ln_g, (l, 2)), _picked(ln_b, (l, 2))],
        out_specs=pl.BlockSpec((tm, D_MODEL), tok),
        out_shape=jax.ShapeDtypeStruct((n, D_MODEL), F32),
        scratch_shapes=[pltpu.VMEM((tm, D_MODEL), F32)],
        compiler_params=_params(("arbitrary",)),
        name="out_ffn",
    )(x2, *parts, w_out, ln_g, ln_b, w_gu, w_down_half, ln_g, ln_b)


def _proj_c_kernel(x_ref, wqk_ref, wvt_ref, wot_ref, wif_ref, wift_ref, cw_ref, cb_ref, gb_ref, gbt_ref,
                   q_ref, k_ref, vt_ref, ogt_ref, gates_ref, gatest_ref, *pre_refs):
    t = pl.program_id(1)
    tm = x_ref.shape[1]
    xb = x_ref[0].astype(BF16)
    width = pre_refs[0].shape[1]

    @pl.when(t == 0)
    def _():
        for pre_ref in pre_refs:
            pre_ref[:SUBLANES, :] = jnp.zeros((SUBLANES, width), F32)

    gates_ref[0] = _dot(xb, wif_ref[...]) + gb_ref[...]
    gatest_ref[0] = _dot_nt(wift_ref[...], xb) + gbt_ref[...]
    for c, pre_ref in enumerate(pre_refs):
        pre_ref[SUBLANES:, :] = _dot(xb, wqk_ref[:, c * width:(c + 1) * width])
    for c, pre_ref in enumerate(pre_refs):
        cols = slice(c * width, (c + 1) * width)
        vt_ref[0, cols, :] = _dot_nt(wvt_ref[cols, :], xb).astype(BF16)
        conv = cb_ref[:, cols] + cw_ref[CONV_W - 1:CONV_W, cols] * pre_ref[SUBLANES:, :]
        for d in range(1, CONV_W):
            conv = conv + cw_ref[CONV_W - 1 - d:CONV_W - d, cols] * pre_ref[SUBLANES - d:SUBLANES - d + tm, :]
        pre_ref[:SUBLANES, :] = pre_ref[tm:, :]
        act = jax.nn.silu(conv)
        if c * width < C_QK_WIDTH:
            q_ref[0, :, cols] = act.astype(BF16)
        else:
            k_cols = slice(c * width - C_QK_WIDTH, (c + 1) * width - C_QK_WIDTH)
            k_ref[0, :, k_cols] = (act * (C_DQK ** -0.5)).astype(BF16)
        ogt_ref[0, cols, :] = jax.nn.sigmoid(_dot_nt(wot_ref[cols, :], xb)).astype(BF16)


def _proj_c(x, wqk, wv_t, wo_t, wif, wift, conv_w, conv_b, gb, gbt):
    bn, s, _ = x.shape
    tm = min(s, PROJ_TILE)
    tok = lambda b, t: (b, t, 0)
    return pl.pallas_call(
        _proj_c_kernel,
        grid=(bn, s // tm),
        in_specs=[pl.BlockSpec((1, tm, D_MODEL), tok)]
        + [_resident(a.shape) for a in (wqk, wv_t, wo_t, wif, wift, conv_w, conv_b, gb, gbt)],
        out_specs=[
            pl.BlockSpec((1, tm, C_QK_WIDTH), tok),
            pl.BlockSpec((1, tm, C_QK_WIDTH), tok),
            pl.BlockSpec((1, C_WIDTH, tm), lambda b, t: (b, 0, t)),
            pl.BlockSpec((1, C_WIDTH, tm), lambda b, t: (b, 0, t)),
            pl.BlockSpec((1, tm, LANES), tok),
            pl.BlockSpec((1, SUBLANES, tm), lambda b, t: (b, 0, t)),
        ],
        out_shape=[
            jax.ShapeDtypeStruct((bn, s, C_QK_WIDTH), BF16),
            jax.ShapeDtypeStruct((bn, s, C_QK_WIDTH), BF16),
            jax.ShapeDtypeStruct((bn, C_WIDTH, s), BF16),
            jax.ShapeDtypeStruct((bn, C_WIDTH, s), BF16),
            jax.ShapeDtypeStruct((bn, s, LANES), F32),
            jax.ShapeDtypeStruct((bn, SUBLANES, s), F32),
        ],
        scratch_shapes=[pltpu.VMEM((SUBLANES + tm, PROJ_CHUNK), F32)] * (2 * C_QK_WIDTH // PROJ_CHUNK),
        compiler_params=_params(("arbitrary", "arbitrary")),
        name="proj_c",
    )(x, wqk, wv_t, wo_t, wif, wift, conv_w, conv_b, gb, gbt)


def _split3(a):
    hi = a.astype(BF16)
    r1 = a - hi.astype(F32)
    mid = r1.astype(BF16)
    lo = (r1 - mid.astype(F32)).astype(BF16)
    return hi, mid, lo


def _mlstm_kernel(q_ref, k_ref, vt_ref, ogt_ref, gates_ref, gatest_ref, hg_ref, o_ref,
                  ct_ref, n_ref, m_ref):
    c_idx = pl.program_id(1)
    L = q_ref.shape[1]

    @pl.when(c_idx == 0)
    def _():
        ct_ref[...] = jnp.zeros_like(ct_ref)
        n_ref[...] = jnp.zeros_like(n_ref)
        m_ref[...] = jnp.zeros_like(m_ref)

    src_idx = lax.broadcasted_iota(jnp.int32, (L, L), 0)
    tgt_idx = lax.broadcasted_iota(jnp.int32, (L, L), 1)
    causal = src_idx <= tgt_idx
    tri = (tgt_idx <= src_idx).astype(BF16)
    tri_t = causal.astype(BF16)

    gates = gates_ref[0]
    gates_t = gatest_ref[0]
    b_cols = sum(_dot(tri, part) for part in _split3(jax.nn.log_sigmoid(gates)))
    b_rows = sum(_dot(part, tri_t) for part in _split3(jax.nn.log_sigmoid(gates_t)))

    for h in range(C_HEADS):
        qk_cols = slice(h * C_DQK, (h + 1) * C_DQK)
        v_rows = slice(h * C_DV, (h + 1) * C_DV)
        c_col = gates[:, h:h + 1] - b_cols[:, C_HEADS + h:C_HEADS + h + 1]
        i_row = gates_t[h:h + 1, :]
        b_row = b_rows[C_HEADS + h:C_HEADS + h + 1, :]
        b_end = b_row[:, L - 1:L]
        m_in = m_ref[h][:, 0:1]
        ct_in = ct_ref[h]
        n_in = n_ref[h]

        masked_c = jnp.where(causal, c_col, -jnp.inf)
        m_row = jnp.maximum(m_in, jnp.max(masked_c, axis=0, keepdims=True))
        p_t = jnp.exp(masked_c - m_row)
        inter = jnp.exp(m_in - m_row)

        q_t = q_ref[0, :, qk_cols].astype(F32).T.astype(BF16)
        kh = k_ref[0, :, qk_cols]
        vt_h = vt_ref[0, v_rows, :]
        sqk_t = _dot(kh, q_t) * p_t
        num_t = _dot(vt_h, sqk_t.astype(BF16)) + inter * _dot(ct_in.astype(BF16), q_t)
        n_q = _dot(jnp.broadcast_to(n_in, (SUBLANES, C_DQK)).astype(BF16), q_t)[0:1]
        den = jnp.sum(sqk_t, axis=0, keepdims=True) + inter * n_q
        hid_t = num_t * (1.0 / jnp.maximum(jnp.abs(den), jnp.exp(-(b_row + m_row))))
        mu = jnp.mean(hid_t, axis=0, keepdims=True)
        zc = hid_t - mu
        var = jnp.mean(zc * zc, axis=0, keepdims=True)
        out_t = ogt_ref[0, v_rows, :].astype(F32) * (zc * lax.rsqrt(var + LN_EPS) * hg_ref[v_rows, :])
        o_ref[0, :, v_rows] = out_t.T.astype(BF16)

        a_row = b_end - b_row + i_row
        a_max = jnp.max(a_row, axis=1, keepdims=True)
        w_end = jnp.exp(a_row - a_max)
        m_new = jnp.maximum(b_end + m_in, a_max)
        decay = jnp.exp(b_end + m_in - m_new)
        inject = jnp.exp(a_max - m_new)
        vw = (vt_h.astype(F32) * w_end).astype(BF16)
        ct_ref[h] = decay * ct_in + inject * _dot(vw, kh)
        k_sum = _dot(jnp.broadcast_to(w_end, (SUBLANES, L)).astype(BF16), kh)[0:1]
        n_ref[h] = decay * n_in + inject * k_sum
        m_ref[h] = jnp.broadcast_to(m_new, (1, LANES))


def _mlstm(q, k, v_t, og_t, gates, gates_t, head_g):
    bn, s, _ = q.shape
    L = MLSTM_L
    tok = lambda b, c: (b, c, 0)
    feat = lambda b, c: (b, 0, c)
    hg_cols = jnp.broadcast_to(head_g.reshape(C_WIDTH, 1), (C_WIDTH, L))
    return pl.pallas_call(
        _mlstm_kernel,
        grid=(bn, s // L),
        in_specs=[
            pl.BlockSpec((1, L, C_QK_WIDTH), tok),
            pl.BlockSpec((1, L, C_QK_WIDTH), tok),
            pl.BlockSpec((1, C_WIDTH, L), feat),
            pl.BlockSpec((1, C_WIDTH, L), feat),
            pl.BlockSpec((1, L, LANES), tok),
            pl.BlockSpec((1, SUBLANES, L), feat),
            _resident((C_WIDTH, L)),
        ],
        out_specs=pl.BlockSpec((1, L, C_WIDTH), tok),
        out_shape=jax.ShapeDtypeStruct((bn, s, C_WIDTH), BF16),
        scratch_shapes=[
            pltpu.VMEM((C_HEADS, C_DV, C_DQK), F32),
            pltpu.VMEM((C_HEADS, 1, C_DQK), F32),
            pltpu.VMEM((C_HEADS, 1, LANES), F32),
        ],
        compiler_params=_params(("arbitrary", "arbitrary")),
        name="mlstm",
    )(q, k, v_t, og_t, gates, gates_t, hg_cols)


def _rope_tables(s):
    half = A_DH // 2
    inv = ROPE_THETA ** (-jnp.arange(half, dtype=F32) / half)
    ang = jnp.arange(s).astype(F32)[:, None] * inv[None, :]
    cos, sin = jnp.cos(ang), jnp.sin(ang)
    cos = jnp.tile(cos, (1, LANES // half))
    sin = jnp.tile(jnp.concatenate([-sin, sin], axis=1), (1, LANES // A_DH))
    return cos, sin, cos.T, sin.T


def _mixer_ab(x, w_in, sgu_ln_g, sgu_ln_b, sgu_w, sgu_b):
    bn, s, d = x.shape
    cos, sin, cos_t, sin_t = _rope_tables(s)
    w_kuv = jnp.concatenate([w_in[:, A_WIDTH:2 * A_WIDTH], w_in[:, 3 * A_WIDTH:]], axis=1).astype(BF16)
    w_qv_t = jnp.concatenate([w_in[:, :A_WIDTH], w_in[:, 2 * A_WIDTH:3 * A_WIDTH]], axis=1).T.astype(BF16)
    bs_full = jnp.repeat(sgu_b.T, B_DG, axis=1)
    q_t, k, v_t, bg = _proj_ab(x, w_kuv, w_qv_t, cos, sin, cos_t, sin_t,
                               sgu_ln_g.reshape(1, B_WIDTH), sgu_ln_b.reshape(1, B_WIDTH), sgu_w, bs_full)
    a = _moba(q_t, k, v_t, _moba_gate(q_t, k))
    return [a.reshape(bn * s, A_WIDTH), bg.reshape(bn * s, B_WIDTH)]


def _mixer_c(x, w_in, conv_w, conv_b, b_i, b_f, head_g):
    bn, s, d = x.shape
    qk_w = 2 * C_QK_WIDTH
    wqk = w_in[:, :qk_w].astype(BF16)
    wv_t = w_in[:, qk_w:qk_w + C_WIDTH].T.astype(BF16)
    wo_t = w_in[:, qk_w + C_WIDTH:qk_w + 2 * C_WIDTH].T.astype(BF16)
    w_if = w_in[:, qk_w + 2 * C_WIDTH:]
    wif = jnp.pad(w_if, ((0, 0), (0, LANES - 2 * C_HEADS))).astype(BF16)
    wift = w_if.T.astype(BF16)
    gate_bias = jnp.concatenate([b_i, b_f]).astype(F32)
    gb = jnp.pad(gate_bias, (0, LANES - 2 * C_HEADS)).reshape(1, LANES)
    gbt = gate_bias.reshape(2 * C_HEADS, 1)
    q, k, v_t, og_t, gates, gates_t = _proj_c(x, wqk, wv_t, wo_t, wif, wift, conv_w,
                                              conv_b.reshape(1, qk_w), gb, gbt)
    hg = _mlstm(q, k, v_t, og_t, gates, gates_t, head_g)
    return [hg.reshape(bn * s, C_WIDTH)]


def kernel(x, ln_g, ln_b, ffn_w_gu, ffn_w_down, ab_w_in, sgu_ln_g, sgu_ln_b, sgu_w, sgu_b, ab_w_out,
           c_w_in, c_conv_w, c_conv_b, c_b_i, c_b_f, c_head_g, c_w_out):
    bn, s, d = x.shape
    n = bn * s

    w_gu = ffn_w_gu.astype(BF16)
    w_down_half = (0.5 * ffn_w_down).astype(BF16)
    ln_g = ln_g.reshape(DEPTH, 3, 1, d)
    ln_b = ln_b.reshape(DEPTH, 3, 1, d)
    x = x.reshape(n, d)
    for l in range(DEPTH):
        x = _ffn_ln(x, w_gu, w_down_half, ln_g, ln_b, l, 0, 0)
        j = l // 2
        if l % 2 == 0:
            parts = _mixer_ab(x.reshape(bn, s, d), ab_w_in[j], sgu_ln_g[j], sgu_ln_b[j], sgu_w[j], sgu_b[j])
            w_out = ab_w_out[j]
        else:
            parts = _mixer_c(x.reshape(bn, s, d), c_w_in[j], c_conv_w[j], c_conv_b[j], c_b_i[j], c_b_f[j],
                             c_head_g[j])
            w_out = c_w_out[j]
        x = _out_ffn(x, parts, w_out.astype(BF16), w_gu, w_down_half, ln_g, ln_b, l)
    return x.reshape(bn, s, d)
```

```python
import functools

import jax
import jax.numpy as jnp
from jax import lax
from jax.experimental import pallas as pl
from jax.experimental.pallas import tpu as pltpu

D_MODEL = 1024
DEPTH = 2
D_FF = 2816
ALPHA = (2 * DEPTH) ** 0.25
LN_EPS = 1e-5
NEG = -1e30

A_HEADS = 8
A_DH = 64
A_WIDTH = A_HEADS * A_DH
MOBA_BLOCK = 256
MOBA_TOPK = 3
ROPE_THETA = 10000.0

B_GROUPS = 8
B_DG = 64
B_WIDTH = B_GROUPS * B_DG
SGU_CHUNK = 128

C_HEADS = 4
C_DQK = 128
C_DV = 256
C_QK_WIDTH = C_HEADS * C_DQK
C_WIDTH = C_HEADS * C_DV
CONV_W = 4

LANES = 128
SUBLANES = 8
VMEM_LIMIT = 56 * 1024 * 1024

TOKEN_TILE = 512
PROJ_TILE = 1024
FF_CHUNK = 256
PROJ_CHUNK = 256
MLSTM_L = 256
MOBA_RING = 3
MOBA_PAIRS = 4
MOBA_SUM_ROWS = 16
MOBA_GATE_WIDTH = 1024
LOG2E = 1.4426950408889634

BF16 = jnp.bfloat16
F32 = jnp.float32


def _params(sem):
    return pltpu.CompilerParams(dimension_semantics=sem, vmem_limit_bytes=VMEM_LIMIT)


def _dot(a, b):
    return jnp.dot(a, b, preferred_element_type=F32)


def _dot_nt(a, b):
    return lax.dot_general(a, b, (((1,), (1,)), ((), ())), preferred_element_type=F32)


def _layer_norm(z, g, b):
    mu = jnp.mean(z, axis=-1, keepdims=True)
    zc = z - mu
    var = jnp.mean(zc * zc, axis=-1, keepdims=True)
    y = zc * lax.rsqrt(var + LN_EPS) * g
    return y if b is None else y + b


def _resident(shape):
    return pl.BlockSpec(shape, lambda *_: (0,) * len(shape), pipeline_mode=pl.Buffered(1))


def _picked(arr, lead):
    rest = arr.shape[len(lead):]
    return pl.BlockSpec((None,) * len(lead) + rest, lambda *_: tuple(lead) + (0,) * len(rest),
                        pipeline_mode=pl.Buffered(1))


def _ffn_ln_value(x, wgu_ref, wd_ref, g_ref, b_ref, acc_ref):
    xb = x.astype(BF16)
    for c in range(D_FF // FF_CHUNK):
        lo = c * FF_CHUNK
        gate = _dot(xb, wgu_ref[:, lo:lo + FF_CHUNK])
        up = _dot(xb, wgu_ref[:, D_FF + lo:D_FF + lo + FF_CHUNK])
        h = (jax.nn.silu(gate) * up).astype(BF16)
        part = _dot(h, wd_ref[lo:lo + FF_CHUNK, :])
        if c == 0:
            acc_ref[...] = part
        else:
            acc_ref[...] += part
    z = ALPHA * x + acc_ref[...]
    return _layer_norm(z, g_ref[...], b_ref[...])


def _ffn_ln_kernel(x_ref, wgu_ref, wd_ref, g_ref, b_ref, o_ref, acc_ref):
    o_ref[...] = _ffn_ln_value(x_ref[...], wgu_ref, wd_ref, g_ref, b_ref, acc_ref)


def _ffn_ln(x2, w_gu, w_down_half, ln_g, ln_b, l, j, ln_idx):
    n = x2.shape[0]
    tm = TOKEN_TILE
    return pl.pallas_call(
        _ffn_ln_kernel,
        grid=(n // tm,),
        in_specs=[pl.BlockSpec((tm, D_MODEL), lambda i: (i, 0)),
                  _picked(w_gu, (l, j)), _picked(w_down_half, (l, j)),
                  _picked(ln_g, (l, ln_idx)), _picked(ln_b, (l, ln_idx))],
        out_specs=pl.BlockSpec((tm, D_MODEL), lambda i: (i, 0)),
        out_shape=jax.ShapeDtypeStruct((n, D_MODEL), F32),
        scratch_shapes=[pltpu.VMEM((tm, D_MODEL), F32)],
        compiler_params=_params(("arbitrary",)),
        name="ffn_ln",
    )(x2, w_gu, w_down_half, ln_g, ln_b)


def _swap_halves_lanes(t, first_half):
    fwd = pltpu.roll(t, A_DH // 2, axis=1)
    bwd = pltpu.roll(t, LANES - A_DH // 2, axis=1)
    return jnp.where(first_half, bwd, fwd)


def _swap_halves_rows(t):
    half = A_DH // 2
    return jnp.concatenate([t[half:2 * half], t[:half], t[3 * half:], t[2 * half:3 * half]], axis=0)


def _sgu_tile(u_ref, vn_ref, w_ref, bs_ref, o_ref):
    tm = u_ref.shape[0]
    t_row = lax.broadcasted_iota(jnp.int32, (SGU_CHUNK, SGU_CHUNK), 0)
    t_col = lax.broadcasted_iota(jnp.int32, (SGU_CHUNK, SGU_CHUNK), 1)
    causal = t_col <= t_row
    lane = lax.broadcasted_iota(jnp.int32, (SGU_CHUNK, LANES), 1)
    lo_mask = lane < B_DG
    for pair in range(B_WIDTH // LANES):
        w_lo = jnp.where(causal, w_ref[2 * pair], 0.0).astype(BF16)
        w_hi = jnp.where(causal, w_ref[2 * pair + 1], 0.0).astype(BF16)
        cols = slice(pair * LANES, (pair + 1) * LANES)
        bias = bs_ref[:, cols]
        for c in range(tm // SGU_CHUNK):
            rows = slice(c * SGU_CHUNK, (c + 1) * SGU_CHUNK)
            vp = vn_ref[rows, cols]
            zero = jnp.zeros_like(vp)
            mixed = _dot(w_lo, jnp.where(lo_mask, vp, zero)) + _dot(w_hi, jnp.where(lo_mask, zero, vp))
            o_ref[0, rows, cols] = (u_ref[rows, cols].astype(F32) * (mixed + bias)).astype(BF16)


def _proj_ab_kernel(x_ref, w_ref, wqvt_ref, cos_ref, sin_ref, cost_ref, sint_ref, lng_ref, lnb_ref,
                    sw_ref, sb_ref, qt_ref, k_ref, vt_ref, bg_ref, u_ref, vn_ref):
    xb = x_ref[0].astype(BF16)
    cos = cos_ref[...]
    sin = sin_ref[...]
    cos_t = cost_ref[...]
    sin_t = sint_ref[...]
    lane = lax.broadcasted_iota(jnp.int32, cos.shape, 1)
    first_half = (lane % A_DH) < (A_DH // 2)
    ub = _dot(xb, w_ref[:, A_WIDTH:A_WIDTH + B_WIDTH])
    u_ref[...] = jax.nn.gelu(ub).astype(BF16)
    vb = _dot(xb, w_ref[:, A_WIDTH + B_WIDTH:])
    vn_ref[...] = _layer_norm(jax.nn.gelu(vb), lng_ref[...], lnb_ref[...]).astype(BF16)
    _sgu_tile(u_ref, vn_ref, sw_ref, sb_ref, bg_ref)
    q_t = _dot_nt(wqvt_ref[:A_WIDTH, :], xb)
    for grp in range(A_WIDTH // LANES):
        lo = grp * LANES
        qg = q_t[lo:lo + LANES]
        qg = qg * cos_t + _swap_halves_rows(qg) * sin_t
        qt_ref[0, lo:lo + LANES, :] = (qg * (A_DH ** -0.5 * LOG2E)).astype(BF16)
        k = _dot(xb, w_ref[:, lo:lo + LANES])
        k = k * cos + _swap_halves_lanes(k, first_half) * sin
        k_ref[0, :, lo:lo + LANES] = k.astype(BF16)
    vt_ref[0] = _dot_nt(wqvt_ref[A_WIDTH:, :], xb).astype(BF16)


def _proj_ab(x, w_kuv, w_qv_t, cos, sin, cos_t, sin_t, lng, lnb, sgu_w, bs_full):
    bn, s, _ = x.shape
    tm = min(s, PROJ_TILE)
    tok = lambda b, t: (b, t, 0)
    tok_t = lambda b, t: (b, 0, t)
    out = jax.ShapeDtypeStruct((bn, s, A_WIDTH), BF16)
    out_t = jax.ShapeDtypeStruct((bn, A_WIDTH, s), BF16)
    return pl.pallas_call(
        _proj_ab_kernel,
        grid=(bn, s // tm),
        in_specs=[
            pl.BlockSpec((1, tm, D_MODEL), tok),
            _resident(w_kuv.shape),
            _resident(w_qv_t.shape),
            pl.BlockSpec((tm, LANES), lambda b, t: (t, 0)),
            pl.BlockSpec((tm, LANES), lambda b, t: (t, 0)),
            pl.BlockSpec((LANES, tm), lambda b, t: (0, t)),
            pl.BlockSpec((LANES, tm), lambda b, t: (0, t)),
            _resident(lng.shape),
            _resident(lnb.shape),
            _resident(sgu_w.shape),
            _resident(bs_full.shape),
        ],
        out_specs=[pl.BlockSpec((1, A_WIDTH, tm), tok_t), pl.BlockSpec((1, tm, A_WIDTH), tok),
                   pl.BlockSpec((1, A_WIDTH, tm), tok_t), pl.BlockSpec((1, tm, B_WIDTH), tok)],
        out_shape=[out_t, out, out_t, out],
        scratch_shapes=[pltpu.VMEM((tm, B_WIDTH), BF16)] * 2,
        compiler_params=_params(("arbitrary", "arbitrary")),
        name="proj_ab",
    )(x, w_kuv, w_qv_t, cos, sin, cos_t, sin_t, lng, lnb, sgu_w, bs_full)


def _moba_gate_kernel(qt_ref, k_ref, avg_ref, bias_ref):
    blk = MOBA_BLOCK
    gate_rows = avg_ref.shape[0]
    s_len = qt_ref.shape[2]
    width = min(s_len, MOBA_GATE_WIDTH)
    kmean = _dot(avg_ref[...], k_ref[0])
    lane_g = lax.broadcasted_iota(jnp.int32, kmean.shape, 1)
    blk_row = lax.broadcasted_iota(jnp.int32, (gate_rows, width), 0)
    feat_row = lax.broadcasted_iota(jnp.int32, (LANES, width), 0)
    qry_lane = lax.broadcasted_iota(jnp.int32, (1, width), 1)
    for c in range(s_len // width):
        q_pair = qt_ref[0, :, c * width:(c + 1) * width]
        q_blk = (qry_lane + c * width) // blk
        for h in range(2):
            head_rows = (feat_row < A_DH) if h == 0 else (feat_row >= A_DH)
            head_lanes = (lane_g < A_DH) if h == 0 else (lane_g >= A_DH)
            qh = jnp.where(head_rows, q_pair, jnp.zeros_like(q_pair))
            gate = _dot(jnp.where(head_lanes, kmean, 0.0).astype(BF16), qh)
            gate = jnp.where(blk_row < q_blk, gate, NEG)
            chosen = jnp.zeros((gate_rows, width), jnp.bool_)
            for _ in range(MOBA_TOPK):
                best = jnp.max(gate, axis=0, keepdims=True)
                idx = jnp.min(jnp.where(gate == best, blk_row, LANES), axis=0, keepdims=True)
                pick = blk_row == idx
                chosen = chosen | (pick & (idx < q_blk))
                gate = jnp.where(pick, -jnp.inf, gate)
            bias_ref[0, h, :, c * width:(c + 1) * width] = jnp.where(chosen, 0.0, NEG).astype(BF16)


def _moba_gate(q_t, k):
    bn, s, _ = k.shape
    blk = MOBA_BLOCK
    nb = s // blk
    assert nb <= LANES
    gate_rows = -(-nb // 16) * 16
    block_avg = ((jnp.arange(gate_rows)[:, None] == jnp.arange(s)[None, :] // blk) / blk).astype(BF16)
    return pl.pallas_call(
        _moba_gate_kernel,
        grid=(bn, A_WIDTH // LANES),
        in_specs=[
            pl.BlockSpec((1, LANES, s), lambda b, p: (b, p, 0)),
            pl.BlockSpec((1, s, LANES), lambda b, p: (b, 0, p)),
            _resident((gate_rows, s)),
        ],
        out_specs=pl.BlockSpec((1, 2, gate_rows, s), lambda b, p: (b, p, 0, 0)),
        out_shape=jax.ShapeDtypeStruct((bn, A_HEADS, gate_rows, s), BF16),
        compiler_params=_params(("arbitrary", "arbitrary")),
        name="moba_gate",
    )(q_t, k, block_avg)


def _moba_kernel(qt_ref, k_ref, vt_ref, bias_ref, e_ref, o_ref, *ring):
    i = pl.program_id(2)
    blk = MOBA_BLOCK
    nb = k_ref.shape[1] // blk
    pairs = qt_ref.shape[1] // LANES
    depth = len(ring) // pairs
    ahead = depth - 2
    gate_rows = bias_ref.shape[2]
    feat_row = lax.broadcasted_iota(jnp.int32, (LANES, blk), 0)
    key_idx = lax.broadcasted_iota(jnp.int32, (blk, blk), 0)
    qry_idx = lax.broadcasted_iota(jnp.int32, (blk, blk), 1)
    causal = key_idx <= qry_idx
    bias_pad = jnp.zeros((LANES - gate_rows, blk), BF16)
    ones_rows = jnp.ones((MOBA_SUM_ROWS, blk), BF16)
    own = pl.multiple_of(i * blk, blk)

    def key_start(j):
        return pl.multiple_of(jnp.minimum(j, nb - 1) * blk, blk)

    def values(head, start):
        return jnp.concatenate([vt_ref[0, head * A_DH:(head + 1) * A_DH, pl.ds(start, blk)], ones_rows], axis=0)

    def absorb(pair, s_both, start, carry, mask=None):
        out = []
        for h in range(2):
            s = s_both[:, h * blk:(h + 1) * blk]
            if mask is not None:
                s = jnp.where(mask, s, NEG)
            s = s.astype(BF16)
            m_new = jnp.max(s, axis=0, keepdims=True).astype(F32)
            pv = lambda p: _dot(values(2 * pair + h, start), p)
            if carry is None:
                out += [m_new, pv(jnp.exp2(s - m_new.astype(BF16)))]
            else:
                m_old, a_old = carry[2 * h], carry[2 * h + 1]
                m_new = jnp.maximum(m_old, m_new)
                out += [m_new, jnp.exp2(m_old - m_new) * a_old + pv(jnp.exp2(s - m_new.astype(BF16)))]
        return out

    rhs, carry = [], []
    for pair in range(pairs):
        rows = slice(pair * LANES, (pair + 1) * LANES)
        q_pair = qt_ref[0, rows, :]
        qh = [jnp.where(feat_row < A_DH, q_pair, jnp.zeros_like(q_pair)),
              jnp.where(feat_row >= A_DH, q_pair, jnp.zeros_like(q_pair))]
        rhs.append(jnp.concatenate([jnp.concatenate([qh[h], bias_ref[0, 2 * pair + h], bias_pad], axis=0)
                                    for h in range(2)], axis=1))
        ring[pair * depth + depth - 1][...] = _dot(k_ref[0, pl.ds(own, blk), rows], jnp.concatenate(qh, axis=1))
    for pair in range(pairs):
        carry += absorb(pair, ring[pair * depth + depth - 1][...], own, None, causal)

    def scores(pair, j):
        start = key_start(j)
        lhs = jnp.concatenate([k_ref[0, pl.ds(start, blk), pair * LANES:(pair + 1) * LANES],
                               e_ref[pl.ds(start, blk), :]], axis=1)
        return _dot(lhs, rhs[pair])

    for pair in range(pairs):
        for r in range(ahead):
            ring[pair * depth + r][...] = scores(pair, r)

    def body(it, carry):
        carry = list(carry)
        for r in range(depth):
            j = depth * it + r
            for pair in range(pairs):
                ring[pair * depth + (r + ahead) % depth][...] = scores(pair, j + ahead)
                carry[4 * pair:4 * pair + 4] = absorb(pair, ring[pair * depth + r][...], key_start(j),
                                                      carry[4 * pair:4 * pair + 4])
        return tuple(carry)

    carry = lax.fori_loop(0, (i + depth - 1) // depth, body, tuple(carry))
    out_t = jnp.concatenate([acc[:A_DH] / acc[A_DH:A_DH + 1] for acc in carry[1::2]], axis=0)
    o_ref[0] = out_t.T.astype(BF16)


def _moba(q_t, k, v_t, bias):
    bn, s, _ = k.shape
    blk = MOBA_BLOCK
    nb = s // blk
    gate_rows = bias.shape[2]
    width = MOBA_PAIRS * LANES
    block_onehot = (jnp.arange(s)[:, None] // blk == jnp.arange(LANES)[None, :]).astype(BF16)
    return pl.pallas_call(
        _moba_kernel,
        grid=(bn, A_WIDTH // width, nb),
        in_specs=[
            pl.BlockSpec((1, width, blk), lambda b, p, i: (b, p, i)),
            pl.BlockSpec((1, s, width), lambda b, p, i: (b, 0, p)),
            pl.BlockSpec((1, width, s), lambda b, p, i: (b, p, 0)),
            pl.BlockSpec((1, 2 * MOBA_PAIRS, gate_rows, blk), lambda b, p, i: (b, p, 0, i)),
            _resident((s, LANES)),
        ],
        out_specs=pl.BlockSpec((1, blk, width), lambda b, p, i: (b, i, p)),
        out_shape=jax.ShapeDtypeStruct((bn, s, A_WIDTH), BF16),
        scratch_shapes=[pltpu.VMEM((blk, 2 * blk), F32)] * (MOBA_PAIRS * MOBA_RING),
        compiler_params=_params(("arbitrary", "arbitrary", "arbitrary")),
        name="moba",
    )(q_t, k, v_t, bias, block_onehot)


def _out_ffn_kernel(n_parts, *refs):
    x_ref = refs[0]
    part_refs = refs[1:1 + n_parts]
    w_ref, g1_ref, b1_ref, wgu_ref, wd_ref, g2_ref, b2_ref, o_ref, acc_ref = refs[1 + n_parts:]
    y = None
    lo = 0
    for p_ref in part_refs:
        width = p_ref.shape[-1]
        term = _dot(p_ref[...], w_ref[lo:lo + width, :])
        y = term if y is None else y + term
        lo += width
    mid = _layer_norm(ALPHA * x_ref[...] + y, g1_ref[...], b1_ref[...])
    o_ref[...] = _ffn_ln_value(mid, wgu_ref, wd_ref, g2_ref, b2_ref, acc_ref)


def _out_ffn(x2, parts, w_out, w_gu, w_down_half, ln_g, ln_b, l):
    n = x2.shape[0]
    tm = TOKEN_TILE
    tok = lambda i: (i, 0)
    return pl.pallas_call(
        functools.partial(_out_ffn_kernel, len(parts)),
        grid=(n // tm,),
        in_specs=[pl.BlockSpec((tm, D_MODEL), tok)]
        + [pl.BlockSpec((tm, p.shape[-1]), tok) for p in parts]
        + [_resident(w_out.shape), _picked(ln_g, (l, 1)), _picked(ln_b, (l, 1)),
           _picked(w_gu, (l, 1)), _picked(w_down_half, (l, 1)), _picked(ln_g, (l, 2)), _picked(ln_b, (l, 2))],
        out_specs=pl.BlockSpec((tm, D_MODEL), tok),
        out_shape=jax.ShapeDtypeStruct((n, D_MODEL), F32),
        scratch_shapes=[pltpu.VMEM((tm, D_MODEL), F32)],
        compiler_params=_params(("arbitrary",)),
        name="out_ffn",
    )(x2, *parts, w_out, ln_g, ln_b, w_gu, w_down_half, ln_g, ln_b)


def _proj_c_kernel(x_ref, wqk_ref, wvt_ref, wot_ref, wif_ref, wift_ref, cw_ref, cb_ref, gb_ref, gbt_ref,
                   q_ref, k_ref, vt_ref, ogt_ref, gates_ref, gatest_ref, *pre_refs):
    t = pl.program_id(1)
    tm = x_ref.shape[1]
    xb = x_ref[0].astype(BF16)
    width = pre_refs[0].shape[1]

    @pl.when(t == 0)
    def _():
        for pre_ref in pre_refs:
            pre_ref[:SUBLANES, :] = jnp.zeros((SUBLANES, width), F32)

    gates_ref[0] = _dot(xb, wif_ref[...]) + gb_ref[...]
    gatest_ref[0] = _dot_nt(wift_ref[...], xb) + gbt_ref[...]
    for c, pre_ref in enumerate(pre_refs):
        pre_ref[SUBLANES:, :] = _dot(xb, wqk_ref[:, c * width:(c + 1) * width])
    for c, pre_ref in enumerate(pre_refs):
        cols = slice(c * width, (c + 1) * width)
        vt_ref[0, cols, :] = _dot_nt(wvt_ref[cols, :], xb).astype(BF16)
        conv = cb_ref[:, cols] + cw_ref[CONV_W - 1:CONV_W, cols] * pre_ref[SUBLANES:, :]
        for d in range(1, CONV_W):
            conv = conv + cw_ref[CONV_W - 1 - d:CONV_W - d, cols] * pre_ref[SUBLANES - d:SUBLANES - d + tm, :]
        pre_ref[:SUBLANES, :] = pre_ref[tm:, :]
        act = jax.nn.silu(conv)
        if c * width < C_QK_WIDTH:
            q_ref[0, :, cols] = act.astype(BF16)
        else:
            k_cols = slice(c * width - C_QK_WIDTH, (c + 1) * width - C_QK_WIDTH)
            k_ref[0, :, k_cols] = (act * (C_DQK ** -0.5)).astype(BF16)
        ogt_ref[0, cols, :] = jax.nn.sigmoid(_dot_nt(wot_ref[cols, :], xb)).astype(BF16)


def _proj_c(x, wqk, wv_t, wo_t, wif, wift, conv_w, conv_b, gb, gbt):
    bn, s, _ = x.shape
    tm = min(s, PROJ_TILE)
    tok = lambda b, t: (b, t, 0)
    return pl.pallas_call(
        _proj_c_kernel,
        grid=(bn, s // tm),
        in_specs=[pl.BlockSpec((1, tm, D_MODEL), tok)]
        + [_resident(a.shape) for a in (wqk, wv_t, wo_t, wif, wift, conv_w, conv_b, gb, gbt)],
        out_specs=[
            pl.BlockSpec((1, tm, C_QK_WIDTH), tok),
            pl.BlockSpec((1, tm, C_QK_WIDTH), tok),
            pl.BlockSpec((1, C_WIDTH, tm), lambda b, t: (b, 0, t)),
            pl.BlockSpec((1, C_WIDTH, tm), lambda b, t: (b, 0, t)),
            pl.BlockSpec((1, tm, LANES), tok),
            pl.BlockSpec((1, SUBLANES, tm), lambda b, t: (b, 0, t)),
        ],
        out_shape=[
            jax.ShapeDtypeStruct((bn, s, C_QK_WIDTH), BF16),
            jax.ShapeDtypeStruct((bn, s, C_QK_WIDTH), BF16),
            jax.ShapeDtypeStruct((bn, C_WIDTH, s), BF16),
            jax.ShapeDtypeStruct((bn, C_WIDTH, s), BF16),
            jax.ShapeDtypeStruct((bn, s, LANES), F32),
            jax.ShapeDtypeStruct((bn, SUBLANES, s), F32),
        ],
        scratch_shapes=[pltpu.VMEM((SUBLANES + tm, PROJ_CHUNK), F32)] * (2 * C_QK_WIDTH // PROJ_CHUNK),
        compiler_params=_params(("arbitrary", "arbitrary")),
        name="proj_c",
    )(x, wqk, wv_t, wo_t, wif, wift, conv_w, conv_b, gb, gbt)


def _split3(a):
    hi = a.astype(BF16)
    r1 = a - hi.astype(F32)
    mid = r1.astype(BF16)
    lo = (r1 - mid.astype(F32)).astype(BF16)
    return hi, mid, lo


def _mlstm_kernel(q_ref, k_ref, vt_ref, ogt_ref, gates_ref, gatest_ref, hg_ref, o_ref,
                  ct_ref, n_ref, m_ref):
    c_idx = pl.program_id(1)
    L = q_ref.shape[1]

    @pl.when(c_idx == 0)
    def _():
        ct_ref[...] = jnp.zeros_like(ct_ref)
        n_ref[...] = jnp.zeros_like(n_ref)
        m_ref[...] = jnp.zeros_like(m_ref)

    src_idx = lax.broadcasted_iota(jnp.int32, (L, L), 0)
    tgt_idx = lax.broadcasted_iota(jnp.int32, (L, L), 1)
    causal = src_idx <= tgt_idx
    tri = (tgt_idx <= src_idx).astype(BF16)
    tri_t = causal.astype(BF16)

    gates = gates_ref[0]
    gates_t = gatest_ref[0]
    b_cols = sum(_dot(tri, part) for part in _split3(jax.nn.log_sigmoid(gates)))
    b_rows = sum(_dot(part, tri_t) for part in _split3(jax.nn.log_sigmoid(gates_t)))

    for h in range(C_HEADS):
        qk_cols = slice(h * C_DQK, (h + 1) * C_DQK)
        v_rows = slice(h * C_DV, (h + 1) * C_DV)
        c_col = gates[:, h:h + 1] - b_cols[:, C_HEADS + h:C_HEADS + h + 1]
        i_row = gates_t[h:h + 1, :]
        b_row = b_rows[C_HEADS + h:C_HEADS + h + 1, :]
        b_end = b_row[:, L - 1:L]
        m_in = m_ref[h][:, 0:1]
        ct_in = ct_ref[h]
        n_in = n_ref[h]

        masked_c = jnp.where(causal, c_col, -jnp.inf)
        m_row = jnp.maximum(m_in, jnp.max(masked_c, axis=0, keepdims=True))
        p_t = jnp.exp(masked_c - m_row)
        inter = jnp.exp(m_in - m_row)

        q_t = q_ref[0, :, qk_cols].astype(F32).T.astype(BF16)
        kh = k_ref[0, :, qk_cols]
        vt_h = vt_ref[0, v_rows, :]
        sqk_t = _dot(kh, q_t) * p_t
        num_t = _dot(vt_h, sqk_t.astype(BF16)) + inter * _dot(ct_in.astype(BF16), q_t)
        n_q = _dot(jnp.broadcast_to(n_in, (SUBLANES, C_DQK)).astype(BF16), q_t)[0:1]
        den = jnp.sum(sqk_t, axis=0, keepdims=True) + inter * n_q
        hid_t = num_t * (1.0 / jnp.maximum(jnp.abs(den), jnp.exp(-(b_row + m_row))))
        mu = jnp.mean(hid_t, axis=0, keepdims=True)
        zc = hid_t - mu
        var = jnp.mean(zc * zc, axis=0, keepdims=True)
        out_t = ogt_ref[0, v_rows, :].astype(F32) * (zc * lax.rsqrt(var + LN_EPS) * hg_ref[v_rows, :])
        o_ref[0, :, v_rows] = out_t.T.astype(BF16)

        a_row = b_end - b_row + i_row
        a_max = jnp.max(a_row, axis=1, keepdims=True)
        w_end = jnp.exp(a_row - a_max)
        m_new = jnp.maximum(b_end + m_in, a_max)
        decay = jnp.exp(b_end + m_in - m_new)
        inject = jnp.exp(a_max - m_new)
        vw = (vt_h.astype(F32) * w_end).astype(BF16)
        ct_ref[h] = decay * ct_in + inject * _dot(vw, kh)
        k_sum = _dot(jnp.broadcast_to(w_end, (SUBLANES, L)).astype(BF16), kh)[0:1]
        n_ref[h] = decay * n_in + inject * k_sum
        m_ref[h] = jnp.broadcast_to(m_new, (1, LANES))


def _mlstm(q, k, v_t, og_t, gates, gates_t, head_g):
    bn, s, _ = q.shape
    L = MLSTM_L
    tok = lambda b, c: (b, c, 0)
    feat = lambda b, c: (b, 0, c)
    hg_cols = jnp.broadcast_to(head_g.reshape(C_WIDTH, 1), (C_WIDTH, L))
    return pl.pallas_call(
        _mlstm_kernel,
        grid=(bn, s // L),
        in_specs=[
            pl.BlockSpec((1, L, C_QK_WIDTH), tok),
            pl.BlockSpec((1, L, C_QK_WIDTH), tok),
            pl.BlockSpec((1, C_WIDTH, L), feat),
            pl.BlockSpec((1, C_WIDTH, L), feat),
            pl.BlockSpec((1, L, LANES), tok),
            pl.BlockSpec((1, SUBLANES, L), feat),
            _resident((C_WIDTH, L)),
        ],
        out_specs=pl.BlockSpec((1, L, C_WIDTH), tok),
        out_shape=jax.ShapeDtypeStruct((bn, s, C_WIDTH), BF16),
        scratch_shapes=[
            pltpu.VMEM((C_HEADS, C_DV, C_DQK), F32),
            pltpu.VMEM((C_HEADS, 1, C_DQK), F32),
            pltpu.VMEM((C_HEADS, 1, LANES), F32),
        ],
        compiler_params=_params(("arbitrary", "arbitrary")),
        name="mlstm",
    )(q, k, v_t, og_t, gates, gates_t, hg_cols)


def _rope_tables(s):
    half = A_DH // 2
    inv = ROPE_THETA ** (-jnp.arange(half, dtype=F32) / half)
    ang = jnp.arange(s).astype(F32)[:, None] * inv[None, :]
    cos, sin = jnp.cos(ang), jnp.sin(ang)
    cos = jnp.tile(cos, (1, LANES // half))
    sin = jnp.tile(jnp.concatenate([-sin, sin], axis=1), (1, LANES // A_DH))
    return cos, sin, cos.T, sin.T


def _mixer_ab(x, w_in, sgu_ln_g, sgu_ln_b, sgu_w, sgu_b):
    bn, s, d = x.shape
    cos, sin, cos_t, sin_t = _rope_tables(s)
    w_kuv = jnp.concatenate([w_in[:, A_WIDTH:2 * A_WIDTH], w_in[:, 3 * A_WIDTH:]], axis=1).astype(BF16)
    w_qv_t = jnp.concatenate([w_in[:, :A_WIDTH], w_in[:, 2 * A_WIDTH:3 * A_WIDTH]], axis=1).T.astype(BF16)
    bs_full = jnp.repeat(sgu_b.T, B_DG, axis=1)
    q_t, k, v_t, bg = _proj_ab(x, w_kuv, w_qv_t, cos, sin, cos_t, sin_t,
                               sgu_ln_g.reshape(1, B_WIDTH), sgu_ln_b.reshape(1, B_WIDTH), sgu_w, bs_full)
    a = _moba(q_t, k, v_t, _moba_gate(q_t, k))
    return [a.reshape(bn * s, A_WIDTH), bg.reshape(bn * s, B_WIDTH)]


def _mixer_c(x, w_in, conv_w, conv_b, b_i, b_f, head_g):
    bn, s, d = x.shape
    qk_w = 2 * C_QK_WIDTH
    wqk = w_in[:, :qk_w].astype(BF16)
    wv_t = w_in[:, qk_w:qk_w + C_WIDTH].T.astype(BF16)
    wo_t = w_in[:, qk_w + C_WIDTH:qk_w + 2 * C_WIDTH].T.astype(BF16)
    w_if = w_in[:, qk_w + 2 * C_WIDTH:]
    wif = jnp.pad(w_if, ((0, 0), (0, LANES - 2 * C_HEADS))).astype(BF16)
    wift = w_if.T.astype(BF16)
    gate_bias = jnp.concatenate([b_i, b_f]).astype(F32)
    gb = jnp.pad(gate_bias, (0, LANES - 2 * C_HEADS)).reshape(1, LANES)
    gbt = gate_bias.reshape(2 * C_HEADS, 1)
    q, k, v_t, og_t, gates, gates_t = _proj_c(x, wqk, wv_t, wo_t, wif, wift, conv_w,
                                              conv_b.reshape(1, qk_w), gb, gbt)
    hg = _mlstm(q, k, v_t, og_t, gates, gates_t, head_g)
    return [hg.reshape(bn * s, C_WIDTH)]


def kernel(x, ln_g, ln_b, ffn_w_gu, ffn_w_down, ab_w_in, sgu_ln_g, sgu_ln_b, sgu_w, sgu_b, ab_w_out,
           c_w_in, c_conv_w, c_conv_b, c_b_i, c_b_f, c_head_g, c_w_out):
    bn, s, d = x.shape
    n = bn * s

    w_gu = ffn_w_gu.astype(BF16)
    w_down_half = (0.5 * ffn_w_down).astype(BF16)
    ln_g = ln_g.reshape(DEPTH, 3, 1, d)
    ln_b = ln_b.reshape(DEPTH, 3, 1, d)
    x = x.reshape(n, d)
    for l in range(DEPTH):
        x = _ffn_ln(x, w_gu, w_down_half, ln_g, ln_b, l, 0, 0)
        j = l // 2
        if l % 2 == 0:
            parts = _mixer_ab(x.reshape(bn, s, d), ab_w_in[j], sgu_ln_g[j], sgu_ln_b[j], sgu_w[j], sgu_b[j])
            w_out = ab_w_out[j]
        else:
            parts = _mixer_c(x.reshape(bn, s, d), c_w_in[j], c_conv_w[j], c_conv_b[j], c_b_i[j], c_b_f[j],
                             c_head_g[j])
            w_out = c_w_out[j]
        x = _out_ffn(x, parts, w_out.astype(BF16), w_gu, w_down_half, ln_g, ln_b, l)
    return x.reshape(bn, s, d)
```

```python
import functools

import jax
import jax.numpy as jnp
from jax import lax
from jax.experimental import pallas as pl
from jax.experimental.pallas import tpu as pltpu

D_MODEL = 1024
DEPTH = 2
D_FF = 2816
ALPHA = (2 * DEPTH) ** 0.25
LN_EPS = 1e-5
NEG = -1e30

A_HEADS = 8
A_DH = 64
A_WIDTH = A_HEADS * A_DH
MOBA_BLOCK = 256
MOBA_TOPK = 3
ROPE_THETA = 10000.0

B_GROUPS = 8
B_DG = 64
B_WIDTH = B_GROUPS * B_DG
SGU_CHUNK = 128

C_HEADS = 4
C_DQK = 128
C_DV = 256
C_QK_WIDTH = C_HEADS * C_DQK
C_WIDTH = C_HEADS * C_DV
CONV_W = 4

LANES = 128
SUBLANES = 8
VMEM_LIMIT = 56 * 1024 * 1024

TOKEN_TILE = 512
PROJ_TILE = 1024
FF_CHUNK = 256
PROJ_CHUNK = 256
MLSTM_L = 256
MOBA_RING = 3
MOBA_PAIRS = 4
MOBA_SUM_ROWS = 16
MOBA_GATE_WIDTH = 1024
LOG2E = 1.4426950408889634

BF16 = jnp.bfloat16
F32 = jnp.float32


def _params(sem):
    return pltpu.CompilerParams(dimension_semantics=sem, vmem_limit_bytes=VMEM_LIMIT)


def _dot(a, b):
    return jnp.dot(a, b, preferred_element_type=F32)


def _dot_nt(a, b):
    return lax.dot_general(a, b, (((1,), (1,)), ((), ())), preferred_element_type=F32)


def _layer_norm(z, g, b):
    mu = jnp.mean(z, axis=-1, keepdims=True)
    zc = z - mu
    var = jnp.mean(zc * zc, axis=-1, keepdims=True)
    y = zc * lax.rsqrt(var + LN_EPS) * g
    return y if b is None else y + b


def _resident(shape):
    return pl.BlockSpec(shape, lambda *_: (0,) * len(shape), pipeline_mode=pl.Buffered(1))


def _picked(arr, lead):
    rest = arr.shape[len(lead):]
    return pl.BlockSpec((None,) * len(lead) + rest, lambda *_: tuple(lead) + (0,) * len(rest),
                        pipeline_mode=pl.Buffered(1))


def _ffn_ln_value(x, wgu_ref, wd_ref, g_ref, b_ref, acc_ref):
    xb = x.astype(BF16)
    for c in range(D_FF // FF_CHUNK):
        lo = c * FF_CHUNK
        gate = _dot(xb, wgu_ref[:, lo:lo + FF_CHUNK])
        up = _dot(xb, wgu_ref[:, D_FF + lo:D_FF + lo + FF_CHUNK])
        h = (jax.nn.silu(gate) * up).astype(BF16)
        part = _dot(h, wd_ref[lo:lo + FF_CHUNK, :])
        if c == 0:
            acc_ref[...] = part
        else:
            acc_ref[...] += part
    z = ALPHA * x + acc_ref[...]
    return _layer_norm(z, g_ref[...], b_ref[...])


def _ffn_ln_kernel(x_ref, wgu_ref, wd_ref, g_ref, b_ref, o_ref, acc_ref):
    o_ref[...] = _ffn_ln_value(x_ref[...], wgu_ref, wd_ref, g_ref, b_ref, acc_ref)


def _ffn_ln(x2, w_gu, w_down_half, ln_g, ln_b, l, j, ln_idx):
    n = x2.shape[0]
    tm = TOKEN_TILE
    return pl.pallas_call(
        _ffn_ln_kernel,
        grid=(n // tm,),
        in_specs=[pl.BlockSpec((tm, D_MODEL), lambda i: (i, 0)),
                  _picked(w_gu, (l, j)), _picked(w_down_half, (l, j)),
                  _picked(ln_g, (l, ln_idx)), _picked(ln_b, (l, ln_idx))],
        out_specs=pl.BlockSpec((tm, D_MODEL), lambda i: (i, 0)),
        out_shape=jax.ShapeDtypeStruct((n, D_MODEL), F32),
        scratch_shapes=[pltpu.VMEM((tm, D_MODEL), F32)],
        compiler_params=_params(("arbitrary",)),
        name="ffn_ln",
    )(x2, w_gu, w_down_half, ln_g, ln_b)


def _swap_halves_lanes(t, first_half):
    fwd = pltpu.roll(t, A_DH // 2, axis=1)
    bwd = pltpu.roll(t, LANES - A_DH // 2, axis=1)
    return jnp.where(first_half, bwd, fwd)


def _swap_halves_rows(t):
    half = A_DH // 2
    return jnp.concatenate([t[half:2 * half], t[:half], t[3 * half:], t[2 * half:3 * half]], axis=0)


def _sgu_tile(u_ref, vn_ref, w_ref, bs_ref, o_ref):
    tm = u_ref.shape[0]
    t_row = lax.broadcasted_iota(jnp.int32, (SGU_CHUNK, SGU_CHUNK), 0)
    t_col = lax.broadcasted_iota(jnp.int32, (SGU_CHUNK, SGU_CHUNK), 1)
    causal = t_col <= t_row
    lane = lax.broadcasted_iota(jnp.int32, (SGU_CHUNK, LANES), 1)
    lo_mask = lane < B_DG
    for pair in range(B_WIDTH // LANES):
        w_lo = jnp.where(causal, w_ref[2 * pair], 0.0).astype(BF16)
        w_hi = jnp.where(causal, w_ref[2 * pair + 1], 0.0).astype(BF16)
        cols = slice(pair * LANES, (pair + 1) * LANES)
        bias = bs_ref[:, cols]
        for c in range(tm // SGU_CHUNK):
            rows = slice(c * SGU_CHUNK, (c + 1) * SGU_CHUNK)
            vp = vn_ref[rows, cols]
            zero = jnp.zeros_like(vp)
            mixed = _dot(w_lo, jnp.where(lo_mask, vp, zero)) + _dot(w_hi, jnp.where(lo_mask, zero, vp))
            o_ref[0, rows, cols] = (u_ref[rows, cols].astype(F32) * (mixed + bias)).astype(BF16)


def _proj_ab_kernel(x_ref, w_ref, wqvt_ref, cos_ref, sin_ref, cost_ref, sint_ref, lng_ref, lnb_ref,
                    sw_ref, sb_ref, qt_ref, k_ref, vt_ref, bg_ref, u_ref, vn_ref):
    xb = x_ref[0].astype(BF16)
    cos = cos_ref[...]
    sin = sin_ref[...]
    cos_t = cost_ref[...]
    sin_t = sint_ref[...]
    lane = lax.broadcasted_iota(jnp.int32, cos.shape, 1)
    first_half = (lane % A_DH) < (A_DH // 2)
    ub = _dot(xb, w_ref[:, A_WIDTH:A_WIDTH + B_WIDTH])
    u_ref[...] = jax.nn.gelu(ub).astype(BF16)
    vb = _dot(xb, w_ref[:, A_WIDTH + B_WIDTH:])
    vn_ref[...] = _layer_norm(jax.nn.gelu(vb), lng_ref[...], lnb_ref[...]).astype(BF16)
    _sgu_tile(u_ref, vn_ref, sw_ref, sb_ref, bg_ref)
    q_t = _dot_nt(wqvt_ref[:A_WIDTH, :], xb)
    for grp in range(A_WIDTH // LANES):
        lo = grp * LANES
        qg = q_t[lo:lo + LANES]
        qg = qg * cos_t + _swap_halves_rows(qg) * sin_t
        qt_ref[0, lo:lo + LANES, :] = (qg * (A_DH ** -0.5 * LOG2E)).astype(BF16)
        k = _dot(xb, w_ref[:, lo:lo + LANES])
        k = k * cos + _swap_halves_lanes(k, first_half) * sin
        k_ref[0, :, lo:lo + LANES] = k.astype(BF16)
    vt_ref[0] = _dot_nt(wqvt_ref[A_WIDTH:, :], xb).astype(BF16)


def _proj_ab(x, w_kuv, w_qv_t, cos, sin, cos_t, sin_t, lng, lnb, sgu_w, bs_full):
    bn, s, _ = x.shape
    tm = min(s, PROJ_TILE)
    tok = lambda b, t: (b, t, 0)
    tok_t = lambda b, t: (b, 0, t)
    out = jax.ShapeDtypeStruct((bn, s, A_WIDTH), BF16)
    out_t = jax.ShapeDtypeStruct((bn, A_WIDTH, s), BF16)
    return pl.pallas_call(
        _proj_ab_kernel,
        grid=(bn, s // tm),
        in_specs=[
            pl.BlockSpec((1, tm, D_MODEL), tok),
            _resident(w_kuv.shape),
            _resident(w_qv_t.shape),
            pl.BlockSpec((tm, LANES), lambda b, t: (t, 0)),
            pl.BlockSpec((tm, LANES), lambda b, t: (t, 0)),
            pl.BlockSpec((LANES, tm), lambda b, t: (0, t)),
            pl.BlockSpec((LANES, tm), lambda b, t: (0, t)),
            _resident(lng.shape),
            _resident(lnb.shape),
            _resident(sgu_w.shape),
            _resident(bs_full.shape),
        ],
        out_specs=[pl.BlockSpec((1, A_WIDTH, tm), tok_t), pl.BlockSpec((1, tm, A_WIDTH), tok),
                   pl.BlockSpec((1, A_WIDTH, tm), tok_t), pl.BlockSpec((1, tm, B_WIDTH), tok)],
        out_shape=[out_t, out, out_t, out],
        scratch_shapes=[pltpu.VMEM((tm, B_WIDTH), BF16)] * 2,
        compiler_params=_params(("arbitrary", "arbitrary")),
        name="proj_ab",
    )(x, w_kuv, w_qv_t, cos, sin, cos_t, sin_t, lng, lnb, sgu_w, bs_full)


def _moba_gate_kernel(qt_ref, k_ref, avg_ref, bias_ref):
    blk = MOBA_BLOCK
    gate_rows = avg_ref.shape[0]
    s_len = qt_ref.shape[2]
    width = min(s_len, MOBA_GATE_WIDTH)
    kmean = _dot(avg_ref[...], k_ref[0])
    lane_g = lax.broadcasted_iota(jnp.int32, kmean.shape, 1)
    blk_row = lax.broadcasted_iota(jnp.int32, (gate_rows, width), 0)
    feat_row = lax.broadcasted_iota(jnp.int32, (LANES, width), 0)
    qry_lane = lax.broadcasted_iota(jnp.int32, (1, width), 1)
    for c in range(s_len // width):
        q_pair = qt_ref[0, :, c * width:(c + 1) * width]
        q_blk = (qry_lane + c * width) // blk
        for h in range(2):
            head_rows = (feat_row < A_DH) if h == 0 else (feat_row >= A_DH)
            head_lanes = (lane_g < A_DH) if h == 0 else (lane_g >= A_DH)
            qh = jnp.where(head_rows, q_pair, jnp.zeros_like(q_pair))
            gate = _dot(jnp.where(head_lanes, kmean, 0.0).astype(BF16), qh)
            gate = jnp.where(blk_row < q_blk, gate, NEG)
            chosen = jnp.zeros((gate_rows, width), jnp.bool_)
            for _ in range(MOBA_TOPK):
                best = jnp.max(gate, axis=0, keepdims=True)
                idx = jnp.min(jnp.where(gate == best, blk_row, LANES), axis=0, keepdims=True)
                pick = blk_row == idx
                chosen = chosen | (pick & (idx < q_blk))
                gate = jnp.where(pick, -jnp.inf, gate)
            bias_ref[0, h, :, c * width:(c + 1) * width] = jnp.where(chosen, 0.0, NEG).astype(BF16)


def _moba_gate(q_t, k):
    bn, s, _ = k.shape
    blk = MOBA_BLOCK
    nb = s // blk
    assert nb <= LANES
    gate_rows = -(-nb // 16) * 16
    block_avg = ((jnp.arange(gate_rows)[:, None] == jnp.arange(s)[None, :] // blk) / blk).astype(BF16)
    return pl.pallas_call(
        _moba_gate_kernel,
        grid=(bn, A_WIDTH // LANES),
        in_specs=[
            pl.BlockSpec((1, LANES, s), lambda b, p: (b, p, 0)),
            pl.BlockSpec((1, s, LANES), lambda b, p: (b, 0, p)),
            _resident((gate_rows, s)),
        ],
        out_specs=pl.BlockSpec((1, 2, gate_rows, s), lambda b, p: (b, p, 0, 0)),
        out_shape=jax.ShapeDtypeStruct((bn, A_HEADS, gate_rows, s), BF16),
        compiler_params=_params(("arbitrary", "arbitrary")),
        name="moba_gate",
    )(q_t, k, block_avg)


def _moba_kernel(qt_ref, k_ref, vt_ref, bias_ref, e_ref, o_ref, *ring):
    i = pl.program_id(2)
    blk = MOBA_BLOCK
    nb = k_ref.shape[1] // blk
    pairs = qt_ref.shape[1] // LANES
    depth = len(ring) // pairs
    ahead = depth - 2
    gate_rows = bias_ref.shape[2]
    feat_row = lax.broadcasted_iota(jnp.int32, (LANES, blk), 0)
    key_idx = lax.broadcasted_iota(jnp.int32, (blk, blk), 0)
    qry_idx = lax.broadcasted_iota(jnp.int32, (blk, blk), 1)
    causal = key_idx <= qry_idx
    bias_pad = jnp.zeros((LANES - gate_rows, blk), BF16)
    ones_rows = jnp.ones((MOBA_SUM_ROWS, blk), BF16)
    own = pl.multiple_of(i * blk, blk)

    def key_start(j):
        return pl.multiple_of(jnp.minimum(j, nb - 1) * blk, blk)

    def values(head, start):
        return jnp.concatenate([vt_ref[0, head * A_DH:(head + 1) * A_DH, pl.ds(start, blk)], ones_rows], axis=0)

    def absorb(pair, s_both, start, carry, mask=None):
        out = []
        for h in range(2):
            s = s_both[:, h * blk:(h + 1) * blk]
            if mask is not None:
                s = jnp.where(mask, s, NEG)
            s = s.astype(BF16)
            m_new = jnp.max(s, axis=0, keepdims=True).astype(F32)
            pv = lambda p: _dot(values(2 * pair + h, start), p)
            if carry is None:
                out += [m_new, pv(jnp.exp2(s - m_new.astype(BF16)))]
            else:
                m_old, a_old = carry[2 * h], carry[2 * h + 1]
                m_new = jnp.maximum(m_old, m_new)
                out += [m_new, jnp.exp2(m_old - m_new) * a_old + pv(jnp.exp2(s - m_new.astype(BF16)))]
        return out

    rhs, carry = [], []
    for pair in range(pairs):
        rows = slice(pair * LANES, (pair + 1) * LANES)
        q_pair = qt_ref[0, rows, :]
        qh = [jnp.where(feat_row < A_DH, q_pair, jnp.zeros_like(q_pair)),
              jnp.where(feat_row >= A_DH, q_pair, jnp.zeros_like(q_pair))]
        rhs.append(jnp.concatenate([jnp.concatenate([qh[h], bias_ref[0, 2 * pair + h], bias_pad], axis=0)
                                    for h in range(2)], axis=1))
        ring[pair * depth + depth - 1][...] = _dot(k_ref[0, pl.ds(own, blk), rows], jnp.concatenate(qh, axis=1))
    for pair in range(pairs):
        carry += absorb(pair, ring[pair * depth + depth - 1][...], own, None, causal)

    def scores(pair, j):
        start = key_start(j)
        lhs = jnp.concatenate([k_ref[0, pl.ds(start, blk), pair * LANES:(pair + 1) * LANES],
                               e_ref[pl.ds(start, blk), :]], axis=1)
        return _dot(lhs, rhs[pair])

    for pair in range(pairs):
        for r in range(ahead):
            ring[pair * depth + r][...] = scores(pair, r)

    def body(it, carry):
        carry = list(carry)
        for r in range(depth):
            j = depth * it + r
            for pair in range(pairs):
                ring[pair * depth + (r + ahead) % depth][...] = scores(pair, j + ahead)
                carry[4 * pair:4 * pair + 4] = absorb(pair, ring[pair * depth + r][...], key_start(j),
                                                      carry[4 * pair:4 * pair + 4])
        return tuple(carry)

    carry = lax.fori_loop(0, (i + depth - 1) // depth, body, tuple(carry))
    out_t = jnp.concatenate([acc[:A_DH] / acc[A_DH:A_DH + 1] for acc in carry[1::2]], axis=0)
    o_ref[0] = out_t.T.astype(BF16)


def _moba(q_t, k, v_t, bias):
    bn, s, _ = k.shape
    blk = MOBA_BLOCK
    nb = s // blk
    gate_rows = bias.shape[2]
    width = MOBA_PAIRS * LANES
    block_onehot = (jnp.arange(s)[:, None] // blk == jnp.arange(LANES)[None, :]).astype(BF16)
    return pl.pallas_call(
        _moba_kernel,
        grid=(bn, A_WIDTH // width, nb),
        in_specs=[
            pl.BlockSpec((1, width, blk), lambda b, p, i: (b, p, i)),
            pl.BlockSpec((1, s, width), lambda b, p, i: (b, 0, p)),
            pl.BlockSpec((1, width, s), lambda b, p, i: (b, p, 0)),
            pl.BlockSpec((1, 2 * MOBA_PAIRS, gate_rows, blk), lambda b, p, i: (b, p, 0, i)),
            _resident((s, LANES)),
        ],
        out_specs=pl.BlockSpec((1, blk, width), lambda b, p, i: (b, i, p)),
        out_shape=jax.ShapeDtypeStruct((bn, s, A_WIDTH), BF16),
        scratch_shapes=[pltpu.VMEM((blk, 2 * blk), F32)] * (MOBA_PAIRS * MOBA_RING),
        compiler_params=_params(("arbitrary", "arbitrary", "arbitrary")),
        name="moba",
    )(q_t, k, v_t, bias, block_onehot)


def _out_ffn_kernel(n_parts, *refs):
    x_ref = refs[0]
    part_refs = refs[1:1 + n_parts]
    w_ref, g1_ref, b1_ref = refs[1 + n_parts:4 + n_parts]
    ffn_refs = refs[4 + n_parts:-2]
    o_ref, acc_ref = refs[-2:]
    y = None
    lo = 0
    for p_ref in part_refs:
        width = p_ref.shape[-1]
        term = _dot(p_ref[...], w_ref[lo:lo + width, :])
        y = term if y is None else y + term
        lo += width
    val = _layer_norm(ALPHA * x_ref[...] + y, g1_ref[...], b1_ref[...])
    for f in range(0, len(ffn_refs), 4):
        val = _ffn_ln_value(val, *ffn_refs[f:f + 4], acc_ref)
    o_ref[...] = val


def _out_ffn(x2, parts, w_out, w_gu, w_down_half, ln_g, ln_b, l, ffns):
    n = x2.shape[0]
    tm = TOKEN_TILE
    tok = lambda i: (i, 0)
    ffn_specs, ffn_args = [], []
    for fl, fj, fn in ffns:
        ffn_specs += [_picked(w_gu, (fl, fj)), _picked(w_down_half, (fl, fj)),
                      _picked(ln_g, (fl, fn)), _picked(ln_b, (fl, fn))]
        ffn_args += [w_gu, w_down_half, ln_g, ln_b]
    return pl.pallas_call(
        functools.partial(_out_ffn_kernel, len(parts)),
        grid=(n // tm,),
        in_specs=[pl.BlockSpec((tm, D_MODEL), tok)]
        + [pl.BlockSpec((tm, p.shape[-1]), tok) for p in parts]
        + [_resident(w_out.shape), _picked(ln_g, (l, 1)), _picked(ln_b, (l, 1))] + ffn_specs,
        out_specs=pl.BlockSpec((tm, D_MODEL), tok),
        out_shape=jax.ShapeDtypeStruct((n, D_MODEL), F32),
        scratch_shapes=[pltpu.VMEM((tm, D_MODEL), F32)],
        compiler_params=_params(("arbitrary",)),
        name="out_ffn",
    )(x2, *parts, w_out, ln_g, ln_b, *ffn_args)


def _proj_c_kernel(x_ref, wqk_ref, wvt_ref, wot_ref, wif_ref, wift_ref, cw_ref, cb_ref, gb_ref, gbt_ref,
                   q_ref, k_ref, vt_ref, ogt_ref, gates_ref, gatest_ref, *pre_refs):
    t = pl.program_id(1)
    tm = x_ref.shape[1]
    xb = x_ref[0].astype(BF16)
    width = pre_refs[0].shape[1]

    @pl.when(t == 0)
    def _():
        for pre_ref in pre_refs:
            pre_ref[:SUBLANES, :] = jnp.zeros((SUBLANES, width), F32)

    gates_ref[0] = _dot(xb, wif_ref[...]) + gb_ref[...]
    gatest_ref[0] = _dot_nt(wift_ref[...], xb) + gbt_ref[...]
    for c, pre_ref in enumerate(pre_refs):
        pre_ref[SUBLANES:, :] = _dot(xb, wqk_ref[:, c * width:(c + 1) * width])
    for c, pre_ref in enumerate(pre_refs):
        cols = slice(c * width, (c + 1) * width)
        vt_ref[0, cols, :] = _dot_nt(wvt_ref[cols, :], xb).astype(BF16)
        conv = cb_ref[:, cols] + cw_ref[CONV_W - 1:CONV_W, cols] * pre_ref[SUBLANES:, :]
        for d in range(1, CONV_W):
            conv = conv + cw_ref[CONV_W - 1 - d:CONV_W - d, cols] * pre_ref[SUBLANES - d:SUBLANES - d + tm, :]
        pre_ref[:SUBLANES, :] = pre_ref[tm:, :]
        act = jax.nn.silu(conv)
        if c * width < C_QK_WIDTH:
            q_ref[0, :, cols] = act.astype(BF16)
        else:
            k_cols = slice(c * width - C_QK_WIDTH, (c + 1) * width - C_QK_WIDTH)
            k_ref[0, :, k_cols] = (act * (C_DQK ** -0.5)).astype(BF16)
        ogt_ref[0, cols, :] = jax.nn.sigmoid(_dot_nt(wot_ref[cols, :], xb)).astype(BF16)


def _proj_c(x, wqk, wv_t, wo_t, wif, wift, conv_w, conv_b, gb, gbt):
    bn, s, _ = x.shape
    tm = min(s, PROJ_TILE)
    tok = lambda b, t: (b, t, 0)
    return pl.pallas_call(
        _proj_c_kernel,
        grid=(bn, s // tm),
        in_specs=[pl.BlockSpec((1, tm, D_MODEL), tok)]
        + [_resident(a.shape) for a in (wqk, wv_t, wo_t, wif, wift, conv_w, conv_b, gb, gbt)],
        out_specs=[
            pl.BlockSpec((1, tm, C_QK_WIDTH), tok),
            pl.BlockSpec((1, tm, C_QK_WIDTH), tok),
            pl.BlockSpec((1, C_WIDTH, tm), lambda b, t: (b, 0, t)),
            pl.BlockSpec((1, C_WIDTH, tm), lambda b, t: (b, 0, t)),
            pl.BlockSpec((1, tm, LANES), tok),
            pl.BlockSpec((1, SUBLANES, tm), lambda b, t: (b, 0, t)),
        ],
        out_shape=[
            jax.ShapeDtypeStruct((bn, s, C_QK_WIDTH), BF16),
            jax.ShapeDtypeStruct((bn, s, C_QK_WIDTH), BF16),
            jax.ShapeDtypeStruct((bn, C_WIDTH, s), BF16),
            jax.ShapeDtypeStruct((bn, C_WIDTH, s), BF16),
            jax.ShapeDtypeStruct((bn, s, LANES), F32),
            jax.ShapeDtypeStruct((bn, SUBLANES, s), F32),
        ],
        scratch_shapes=[pltpu.VMEM((SUBLANES + tm, PROJ_CHUNK), F32)] * (2 * C_QK_WIDTH // PROJ_CHUNK),
        compiler_params=_params(("arbitrary", "arbitrary")),
        name="proj_c",
    )(x, wqk, wv_t, wo_t, wif, wift, conv_w, conv_b, gb, gbt)


def _split3(a):
    hi = a.astype(BF16)
    r1 = a - hi.astype(F32)
    mid = r1.astype(BF16)
    lo = (r1 - mid.astype(F32)).astype(BF16)
    return hi, mid, lo


def _mlstm_kernel(q_ref, k_ref, vt_ref, ogt_ref, gates_ref, gatest_ref, hg_ref, o_ref,
                  ct_ref, n_ref, m_ref):
    c_idx = pl.program_id(1)
    L = q_ref.shape[1]

    @pl.when(c_idx == 0)
    def _():
        ct_ref[...] = jnp.zeros_like(ct_ref)
        n_ref[...] = jnp.zeros_like(n_ref)
        m_ref[...] = jnp.zeros_like(m_ref)

    src_idx = lax.broadcasted_iota(jnp.int32, (L, L), 0)
    tgt_idx = lax.broadcasted_iota(jnp.int32, (L, L), 1)
    causal = src_idx <= tgt_idx
    tri = (tgt_idx <= src_idx).astype(BF16)
    tri_t = causal.astype(BF16)

    gates = gates_ref[0]
    gates_t = gatest_ref[0]
    b_cols = sum(_dot(tri, part) for part in _split3(jax.nn.log_sigmoid(gates)))
    b_rows = sum(_dot(part, tri_t) for part in _split3(jax.nn.log_sigmoid(gates_t)))

    for h in range(C_HEADS):
        qk_cols = slice(h * C_DQK, (h + 1) * C_DQK)
        v_rows = slice(h * C_DV, (h + 1) * C_DV)
        c_col = gates[:, h:h + 1] - b_cols[:, C_HEADS + h:C_HEADS + h + 1]
        i_row = gates_t[h:h + 1, :]
        b_row = b_rows[C_HEADS + h:C_HEADS + h + 1, :]
        b_end = b_row[:, L - 1:L]
        m_in = m_ref[h][:, 0:1]
        ct_in = ct_ref[h]
        n_in = n_ref[h]

        masked_c = jnp.where(causal, c_col, -jnp.inf)
        m_row = jnp.maximum(m_in, jnp.max(masked_c, axis=0, keepdims=True))
        p_t = jnp.exp(masked_c - m_row)
        inter = jnp.exp(m_in - m_row)

        q_t = q_ref[0, :, qk_cols].astype(F32).T.astype(BF16)
        kh = k_ref[0, :, qk_cols]
        vt_h = vt_ref[0, v_rows, :]
        sqk_t = _dot(kh, q_t) * p_t
        num_t = _dot(vt_h, sqk_t.astype(BF16)) + inter * _dot(ct_in.astype(BF16), q_t)
        n_q = _dot(jnp.broadcast_to(n_in, (SUBLANES, C_DQK)).astype(BF16), q_t)[0:1]
        den = jnp.sum(sqk_t, axis=0, keepdims=True) + inter * n_q
        hid_t = num_t * (1.0 / jnp.maximum(jnp.abs(den), jnp.exp(-(b_row + m_row))))
        mu = jnp.mean(hid_t, axis=0, keepdims=True)
        zc = hid_t - mu
        var = jnp.mean(zc * zc, axis=0, keepdims=True)
        out_t = ogt_ref[0, v_rows, :].astype(F32) * (zc * lax.rsqrt(var + LN_EPS) * hg_ref[v_rows, :])
        o_ref[0, :, v_rows] = out_t.T.astype(BF16)

        a_row = b_end - b_row + i_row
        a_max = jnp.max(a_row, axis=1, keepdims=True)
        w_end = jnp.exp(a_row - a_max)
        m_new = jnp.maximum(b_end + m_in, a_max)
        decay = jnp.exp(b_end + m_in - m_new)
        inject = jnp.exp(a_max - m_new)
        vw = (vt_h.astype(F32) * w_end).astype(BF16)
        ct_ref[h] = decay * ct_in + inject * _dot(vw, kh)
        k_sum = _dot(jnp.broadcast_to(w_end, (SUBLANES, L)).astype(BF16), kh)[0:1]
        n_ref[h] = decay * n_in + inject * k_sum
        m_ref[h] = jnp.broadcast_to(m_new, (1, LANES))


def _mlstm(q, k, v_t, og_t, gates, gates_t, head_g):
    bn, s, _ = q.shape
    L = MLSTM_L
    tok = lambda b, c: (b, c, 0)
    feat = lambda b, c: (b, 0, c)
    hg_cols = jnp.broadcast_to(head_g.reshape(C_WIDTH, 1), (C_WIDTH, L))
    return pl.pallas_call(
        _mlstm_kernel,
        grid=(bn, s // L),
        in_specs=[
            pl.BlockSpec((1, L, C_QK_WIDTH), tok),
            pl.BlockSpec((1, L, C_QK_WIDTH), tok),
            pl.BlockSpec((1, C_WIDTH, L), feat),
            pl.BlockSpec((1, C_WIDTH, L), feat),
            pl.BlockSpec((1, L, LANES), tok),
            pl.BlockSpec((1, SUBLANES, L), feat),
            _resident((C_WIDTH, L)),
        ],
        out_specs=pl.BlockSpec((1, L, C_WIDTH), tok),
        out_shape=jax.ShapeDtypeStruct((bn, s, C_WIDTH), BF16),
        scratch_shapes=[
            pltpu.VMEM((C_HEADS, C_DV, C_DQK), F32),
            pltpu.VMEM((C_HEADS, 1, C_DQK), F32),
            pltpu.VMEM((C_HEADS, 1, LANES), F32),
        ],
        compiler_params=_params(("arbitrary", "arbitrary")),
        name="mlstm",
    )(q, k, v_t, og_t, gates, gates_t, hg_cols)


def _rope_tables(s):
    half = A_DH // 2
    inv = ROPE_THETA ** (-jnp.arange(half, dtype=F32) / half)
    ang = jnp.arange(s).astype(F32)[:, None] * inv[None, :]
    cos, sin = jnp.cos(ang), jnp.sin(ang)
    cos = jnp.tile(cos, (1, LANES // half))
    sin = jnp.tile(jnp.concatenate([-sin, sin], axis=1), (1, LANES // A_DH))
    return cos, sin, cos.T, sin.T


def _mixer_ab(x, w_in, sgu_ln_g, sgu_ln_b, sgu_w, sgu_b):
    bn, s, d = x.shape
    cos, sin, cos_t, sin_t = _rope_tables(s)
    w_kuv = jnp.concatenate([w_in[:, A_WIDTH:2 * A_WIDTH], w_in[:, 3 * A_WIDTH:]], axis=1).astype(BF16)
    w_qv_t = jnp.concatenate([w_in[:, :A_WIDTH], w_in[:, 2 * A_WIDTH:3 * A_WIDTH]], axis=1).T.astype(BF16)
    bs_full = jnp.repeat(sgu_b.T, B_DG, axis=1)
    q_t, k, v_t, bg = _proj_ab(x, w_kuv, w_qv_t, cos, sin, cos_t, sin_t,
                               sgu_ln_g.reshape(1, B_WIDTH), sgu_ln_b.reshape(1, B_WIDTH), sgu_w, bs_full)
    a = _moba(q_t, k, v_t, _moba_gate(q_t, k))
    return [a.reshape(bn * s, A_WIDTH), bg.reshape(bn * s, B_WIDTH)]


def _mixer_c(x, w_in, conv_w, conv_b, b_i, b_f, head_g):
    bn, s, d = x.shape
    qk_w = 2 * C_QK_WIDTH
    wqk = w_in[:, :qk_w].astype(BF16)
    wv_t = w_in[:, qk_w:qk_w + C_WIDTH].T.astype(BF16)
    wo_t = w_in[:, qk_w + C_WIDTH:qk_w + 2 * C_WIDTH].T.astype(BF16)
    w_if = w_in[:, qk_w + 2 * C_WIDTH:]
    wif = jnp.pad(w_if, ((0, 0), (0, LANES - 2 * C_HEADS))).astype(BF16)
    wift = w_if.T.astype(BF16)
    gate_bias = jnp.concatenate([b_i, b_f]).astype(F32)
    gb = jnp.pad(gate_bias, (0, LANES - 2 * C_HEADS)).reshape(1, LANES)
    gbt = gate_bias.reshape(2 * C_HEADS, 1)
    q, k, v_t, og_t, gates, gates_t = _proj_c(x, wqk, wv_t, wo_t, wif, wift, conv_w,
                                              conv_b.reshape(1, qk_w), gb, gbt)
    hg = _mlstm(q, k, v_t, og_t, gates, gates_t, head_g)
    return [hg.reshape(bn * s, C_WIDTH)]


def kernel(x, ln_g, ln_b, ffn_w_gu, ffn_w_down, ab_w_in, sgu_ln_g, sgu_ln_b, sgu_w, sgu_b, ab_w_out,
           c_w_in, c_conv_w, c_conv_b, c_b_i, c_b_f, c_head_g, c_w_out):
    bn, s, d = x.shape
    n = bn * s

    w_gu = ffn_w_gu.astype(BF16)
    w_down_half = (0.5 * ffn_w_down).astype(BF16)
    ln_g = ln_g.reshape(DEPTH, 3, 1, d)
    ln_b = ln_b.reshape(DEPTH, 3, 1, d)
    x = x.reshape(n, d)
    x = _ffn_ln(x, w_gu, w_down_half, ln_g, ln_b, 0, 0, 0)
    for l in range(DEPTH):
        j = l // 2
        if l % 2 == 0:
            parts = _mixer_ab(x.reshape(bn, s, d), ab_w_in[j], sgu_ln_g[j], sgu_ln_b[j], sgu_w[j], sgu_b[j])
            w_out = ab_w_out[j]
        else:
            parts = _mixer_c(x.reshape(bn, s, d), c_w_in[j], c_conv_w[j], c_conv_b[j], c_b_i[j], c_b_f[j],
                             c_head_g[j])
            w_out = c_w_out[j]
        ffns = [(l, 1, 2)] + ([(l + 1, 0, 0)] if l + 1 < DEPTH else [])
        x = _out_ffn(x, parts, w_out.astype(BF16), w_gu, w_down_half, ln_g, ln_b, l, ffns)
    return x.reshape(bn, s, d)
```
